```python
import math
import jax, jax.numpy as jnp
from jax import lax
import numpy as np

D_MODEL = 1024
BATCH = 16
SEQ = 2048
DEPTH = 4

GRID_W = 64
CTX_LEN = 256
ATTN_HEADS = 4
HEAD_DIM = 64
ATTN_W = ATTN_HEADS * 2 * HEAD_DIM
ATTN_SCALE = HEAD_DIM ** -0.5
RG_W = D_MODEL - ATTN_W
RG_BLOCKS = 8
RG_BW = RG_W // RG_BLOCKS
RG_C = 8.0
CONV_W = 4
CONV_PAD_L = CONV_W // 2
CONV_PAD_R = CONV_W - 1 - CONV_PAD_L
CTX_COLS = 2 * ATTN_W + RG_W
IN_W = CTX_COLS + ATTN_W + RG_W
D_FF = 2816
N_EXPERTS = 8
TOP_K = 2
Q_BLOCK = 128
ROPE_BASE = 10000.0
ROPE_PAIRS = HEAD_DIM // 4
ALPHA = (2 * DEPTH) ** 0.25
BETA = (8 * DEPTH) ** -0.25
LN_EPS = 1e-5
RMS_EPS = 1e-5

kernel_name = 'hybrid_diffattn_rglru_moe_dit'

F32 = jnp.float32


def layer_norm(x, g, b):
    xf = x.astype(F32)
    mu = jnp.mean(xf, -1, keepdims=True)
    var = jnp.mean(jnp.square(xf - mu), -1, keepdims=True)
    return ((xf - mu) * lax.rsqrt(var + LN_EPS) * g + b).astype(x.dtype)


def adaln(cvec, w, b):
    m = (jax.nn.silu(cvec.astype(F32)) @ w + b).astype(cvec.dtype)
    return [t[:, None, :] for t in jnp.split(m, 6, axis=-1)]


def modulate(h, shift, scale):
    return h * (1 + scale) + shift


def axial_rope_tables(rows):
    row = jnp.repeat(jnp.arange(rows, dtype=F32), GRID_W)
    col = jnp.tile(jnp.arange(GRID_W, dtype=F32), rows)
    inv_freq = ROPE_BASE ** (-jnp.arange(ROPE_PAIRS, dtype=F32) / ROPE_PAIRS)
    ang = jnp.concatenate([row[:, None] * inv_freq, col[:, None] * inv_freq], -1)
    return jnp.cos(ang), jnp.sin(ang)


def apply_rope(t, cos, sin):
    tp = t.astype(F32).reshape(*t.shape[:-1], HEAD_DIM // 2, 2)
    cs = cos[None, :, None, None, :]
    sn = sin[None, :, None, None, :]
    te, to = tp[..., 0], tp[..., 1]
    return jnp.stack([te * cs - to * sn, te * sn + to * cs], -1).reshape(t.shape)


def qk_heads(t):
    return t.reshape(*t.shape[:2], ATTN_HEADS, 2, HEAD_DIM)


def v_heads(t):
    return t.reshape(*t.shape[:2], ATTN_HEADS, 2 * HEAD_DIM)


def diff_attn(q, k, v, lam):
    s = jnp.einsum('bqhmd,bkhmd->bhmqk', q, k, preferred_element_type=F32) * ATTN_SCALE
    p = jax.nn.softmax(s, axis=-1)
    a = p[:, :, 0] - lam * p[:, :, 1]
    return jnp.einsum('bhqk,bkhe->bqhe', a, v.astype(F32))


def diff_attn_blocked(q, k, v, lam):
    b, l = q.shape[:2]
    nb = l // Q_BLOCK
    qb = jnp.moveaxis(q.reshape(b, nb, Q_BLOCK, *q.shape[2:]), 1, 0)
    ob = lax.map(lambda qi: diff_attn(qi, k, v, lam), qb)
    return jnp.moveaxis(ob, 0, 1).reshape(b, l, *ob.shape[3:])


def diff_attn_finish(o, g, lam_init, dtype):
    o = o * lax.rsqrt(jnp.mean(o * o, -1, keepdims=True) + RMS_EPS) * g * (1.0 - lam_init)
    return o.reshape(*o.shape[:2], ATTN_W).astype(dtype)


def short_conv(x, w, b):
    l = x.shape[1]
    xp = jnp.pad(x, ((0, 0), (CONV_PAD_L, CONV_PAD_R), (0, 0)))
    y = b + xp[:, 0:l] * w[0]
    for j in range(1, CONV_W):
        y = y + xp[:, j:j + l] * w[j]
    return y


def rglru_coeffs(xc, wa, ba, wx, bx, lam):
    b, l, _ = xc.shape
    xf = xc.astype(F32)
    xb = xf.reshape(b, l, RG_BLOCKS, RG_BW)
    r = jax.nn.sigmoid(jnp.einsum('blgi,gij->blgj', xb, wa).reshape(b, l, RG_W) + ba)
    gi = jax.nn.sigmoid(jnp.einsum('blgi,gij->blgj', xb, wx).reshape(b, l, RG_W) + bx)
    log_a = -RG_C * r * jax.nn.softplus(-lam.astype(F32))
    return jnp.exp(log_a), jnp.sqrt(-jnp.expm1(2.0 * log_a)) * (gi * xf)


def _scan_combine(lhs, rhs):
    return (lhs[0] * rhs[0], rhs[0] * lhs[1] + rhs[1])


def linear_scan(a, bt, h0):
    acum, hcum = lax.associative_scan(_scan_combine, (a, bt), axis=1)
    return hcum + acum * h0[:, None, :]


def bidir_rglru(xr_ctx, xr_lat, conv_w, conv_b, wa, ba, wx, bx, lam, want_ctx):
    xc = short_conv(xr_ctx, conv_w, conv_b)
    xl = short_conv(xr_lat, conv_w, conv_b)
    h0 = jnp.zeros((xc.shape[0], RG_W), F32)
    out_c, out_l = None, 0.0
    for d in range(2):
        a_c, b_c = rglru_coeffs(xc, wa[d], ba[d], wx[d], bx[d], lam[d])
        a_l, b_l = rglru_coeffs(xl, wa[d], ba[d], wx[d], bx[d], lam[d])
        if d == 1:
            a_c, b_c, a_l, b_l = [jnp.flip(t, 1) for t in (a_c, b_c, a_l, b_l)]
        h_c = linear_scan(a_c, b_c, h0)
        h_l = linear_scan(a_l, b_l, h_c[:, -1])
        if d == 1:
            h_c, h_l = jnp.flip(h_c, 1), jnp.flip(h_l, 1)
        out_l = out_l + h_l
        if want_ctx:
            out_c = h_c if out_c is None else out_c + h_c
    return out_c, out_l


def swiglu(u, w1, w3, w2):
    return (jax.nn.silu(u @ w1) * (u @ w3)) @ w2


def moe_swiglu(u, router, w1, w3, w2):
    logits = (u @ router).astype(F32)
    top_v, top_i = lax.top_k(logits, TOP_K)
    gates = jax.nn.softmax(top_v, axis=-1)
    combine = jnp.sum(jax.nn.one_hot(top_i, N_EXPERTS, dtype=F32) * gates[..., None], axis=-2)
    out = jnp.zeros(u.shape, F32)
    for e in range(N_EXPERTS):
        out = out + combine[..., e:e + 1] * swiglu(u, w1[e], w3[e], w2[e]).astype(F32)
    return out.astype(u.dtype)


def setup_inputs(seed: int = 0) -> dict:
    key = jax.random.key(seed)
    ks = iter(jax.random.split(key, 40))
    nrm = lambda shape, s: jax.random.normal(next(ks), shape, F32) * s
    n_dense = (DEPTH + 1) // 2
    n_moe = DEPTH // 2
    u = jax.random.uniform(next(ks), (DEPTH, 2, RG_W), F32, minval=0.9, maxval=0.999)
    a0 = u ** (1.0 / RG_C)
    rg_lambda = jnp.log(a0) - jnp.log1p(-a0)
    return {
        'x': nrm((BATCH, SEQ, D_MODEL), 1.0),
        'c': nrm((BATCH, D_MODEL), 1.0),
        'ctx': nrm((BATCH, CTX_LEN, D_MODEL), 1.0),
        'c_ctx': nrm((D_MODEL,), 1.0),
        'w_mod': nrm((DEPTH, D_MODEL, 6 * D_MODEL), 0.5 * D_MODEL ** -0.5),
        'b_mod': nrm((DEPTH, 6 * D_MODEL), 0.01),
        'w_in': nrm((DEPTH, D_MODEL, IN_W), D_MODEL ** -0.5),
        'lam_q1': nrm((DEPTH, HEAD_DIM), 0.1),
        'lam_k1': nrm((DEPTH, HEAD_DIM), 0.1),
        'lam_q2': nrm((DEPTH, HEAD_DIM), 0.1),
        'lam_k2': nrm((DEPTH, HEAD_DIM), 0.1),
        'subln_g': 1.0 + nrm((DEPTH, 2 * HEAD_DIM), 0.01),
        'conv_w': nrm((DEPTH, CONV_W, RG_W), CONV_W ** -0.5),
        'conv_b': nrm((DEPTH, RG_W), 0.01),
        'rg_wa': nrm((DEPTH, 2, RG_BLOCKS, RG_BW, RG_BW), RG_BW ** -0.5),
        'rg_ba': nrm((DEPTH, 2, RG_W), 0.01),
        'rg_wx': nrm((DEPTH, 2, RG_BLOCKS, RG_BW, RG_BW), RG_BW ** -0.5),
        'rg_bx': nrm((DEPTH, 2, RG_W), 0.01),
        'rg_lambda': rg_lambda,
        'w_out': nrm((DEPTH, D_MODEL, D_MODEL), BETA * D_MODEL ** -0.5),
        'ln1_g': 1.0 + nrm((DEPTH, D_MODEL), 0.01),
        'ln1_b': nrm((DEPTH, D_MODEL), 0.01),
        'ln2_g': 1.0 + nrm((DEPTH, D_MODEL), 0.01),
        'ln2_b': nrm((DEPTH, D_MODEL), 0.01),
        'ffn_w1': nrm((n_dense, D_MODEL, D_FF), D_MODEL ** -0.5),
        'ffn_w3': nrm((n_dense, D_MODEL, D_FF), D_MODEL ** -0.5),
        'ffn_w2': nrm((n_dense, D_FF, D_MODEL), BETA * D_FF ** -0.5),
        'moe_router': nrm((n_moe, D_MODEL, N_EXPERTS), D_MODEL ** -0.5),
        'moe_w1': nrm((n_moe, N_EXPERTS, D_MODEL, D_FF), D_MODEL ** -0.5),
        'moe_w3': nrm((n_moe, N_EXPERTS, D_MODEL, D_FF), D_MODEL ** -0.5),
        'moe_w2': nrm((n_moe, N_EXPERTS, D_FF, D_MODEL), BETA * D_FF ** -0.5),
    }


def reference(x, c, ctx, c_ctx, w_mod, b_mod, w_in, lam_q1, lam_k1, lam_q2, lam_k2, subln_g,
              conv_w, conv_b, rg_wa, rg_ba, rg_wx, rg_bx, rg_lambda, w_out,
              ln1_g, ln1_b, ln2_g, ln2_b, ffn_w1, ffn_w3, ffn_w2,
              moe_router, moe_w1, moe_w3, moe_w2):
    dt = x.dtype
    rows = x.shape[1] // GRID_W
    cos, sin = axial_rope_tables(rows)
    h_lat, h_ctx = x, ctx

    def channel_mix(u, i):
        if i % 2 == 0:
            j = i // 2
            return swiglu(u, ffn_w1[j], ffn_w3[j], ffn_w2[j])
        j = i // 2
        return moe_swiglu(u, moe_router[j], moe_w1[j], moe_w3[j], moe_w2[j])

    for i in range(DEPTH):
        last = i == DEPTH - 1
        sh1, sc1, g1, sh2, sc2, g2 = adaln(c, w_mod[i], b_mod[i])
        csh1, csc1, cg1, csh2, csc2, cg2 = adaln(c_ctx[None], w_mod[i], b_mod[i])
        lam_init = 0.8 - 0.6 * math.exp(-0.3 * i)
        lam = (jnp.exp(jnp.sum(lam_q1[i] * lam_k1[i]).astype(F32))
               - jnp.exp(jnp.sum(lam_q2[i] * lam_k2[i]).astype(F32)) + lam_init)

        p_lat = modulate(h_lat, sh1, sc1) @ w_in[i]
        p_ctx = modulate(h_ctx, csh1, csc1) @ (w_in[i][:, :CTX_COLS] if last else w_in[i])
        k_l = apply_rope(qk_heads(p_lat[..., :ATTN_W]), cos, sin)
        v_l = v_heads(p_lat[..., ATTN_W:2 * ATTN_W])
        xr_l = p_lat[..., 2 * ATTN_W:CTX_COLS]
        q_l = apply_rope(qk_heads(p_lat[..., CTX_COLS:CTX_COLS + ATTN_W]), cos, sin)
        y_l = p_lat[..., CTX_COLS + ATTN_W:]
        k_c = qk_heads(p_ctx[..., :ATTN_W]).astype(F32)
        v_c = v_heads(p_ctx[..., ATTN_W:2 * ATTN_W])
        xr_c = p_ctx[..., 2 * ATTN_W:CTX_COLS]

        k_all = jnp.concatenate([k_c, k_l], axis=1)
        v_all = jnp.concatenate([v_c, v_l], axis=1)
        att_l = diff_attn_finish(diff_attn_blocked(q_l, k_all, v_all, lam), subln_g[i], lam_init, dt)

        rg_c, rg_lat = bidir_rglru(xr_c, xr_l, conv_w[i], conv_b[i], rg_wa[i], rg_ba[i],
                                   rg_wx[i], rg_bx[i], rg_lambda[i], not last)
        rg_l = (rg_lat * jax.nn.gelu(y_l.astype(F32))).astype(dt)

        mix_l = jnp.concatenate([att_l, rg_l], axis=-1) @ w_out[i]
        h_lat = layer_norm(ALPHA * h_lat + (1 + g1) * mix_l, ln1_g[i], ln1_b[i])
        h_lat = layer_norm(ALPHA * h_lat + (1 + g2) * channel_mix(modulate(h_lat, sh2, sc2), i),
                           ln2_g[i], ln2_b[i])

        if not last:
            q_c = qk_heads(p_ctx[..., CTX_COLS:CTX_COLS + ATTN_W]).astype(F32)
            y_c = p_ctx[..., CTX_COLS + ATTN_W:]
            att_c = diff_attn_finish(diff_attn(q_c, k_c, v_c, lam), subln_g[i], lam_init, dt)
            rgc = (rg_c * jax.nn.gelu(y_c.astype(F32))).astype(dt)
            mix_c = jnp.concatenate([att_c, rgc], axis=-1) @ w_out[i]
            h_ctx = layer_norm(ALPHA * h_ctx + (1 + cg1) * mix_c, ln1_g[i], ln1_b[i])
            h_ctx = layer_norm(ALPHA * h_ctx + (1 + cg2) * channel_mix(modulate(h_ctx, csh2, csc2), i),
                               ln2_g[i], ln2_b[i])
    return h_lat
```

```python
import functools
import math

import jax
import jax.numpy as jnp
from jax import lax
from jax.experimental import pallas as pl
from jax.experimental.pallas import tpu as pltpu

F32 = jnp.float32
BF16 = jnp.bfloat16

HEAD_DIM = 64
ATTN_HEADS = 4
HEAD_W = 2 * HEAD_DIM
ATTN_W = ATTN_HEADS * HEAD_W
GRID_W = 64
RG_BLOCKS = 8
RG_C = 8.0
CONV_W = 4
N_TOP = 2
ROPE_BASE = 10000.0
LN_EPS = 1e-5
RMS_EPS = 1e-5

LANES = 128
SUBLANES = 8
MXU_DIM = 256
VMEM_LIMIT = 56 * 1024 * 1024


def _cparams(sem, vmem=VMEM_LIMIT):
    return pltpu.CompilerParams(dimension_semantics=sem, vmem_limit_bytes=vmem)


def _pick_tile(n, pref):
    t = min(n, pref)
    while t > 8 and (n % t or t % 8):
        t -= 8
    assert n % t == 0
    return t


def _layer_norm(z, g, b):
    mu = jnp.mean(z, -1, keepdims=True)
    zc = z - mu
    var = jnp.mean(zc * zc, -1, keepdims=True)
    return zc * lax.rsqrt(var + LN_EPS) * g + b


def _row_mods(mb_ref, mc_ref, idx, is_ctx):
    return jnp.where(is_ctx, mc_ref[0, idx:idx + 1, :], mb_ref[0, idx:idx + 1, :])


def _mod_kernel(c_ref, w_ref, b_ref, o_ref):
    cv = c_ref[...]
    s = (cv * jax.nn.sigmoid(cv)).astype(BF16)
    o_ref[0] = jnp.dot(s, w_ref[0].astype(BF16), preferred_element_type=F32) + b_ref[0]


def _modulation(cvec, w_mod, b_mod):
    depth, d, n = w_mod.shape
    mb = cvec.shape[0]
    tn = _pick_tile(n, 1536)
    return pl.pallas_call(
        _mod_kernel,
        grid=(depth, n // tn),
        in_specs=[
            pl.BlockSpec((mb, d), lambda i, j: (0, 0)),
            pl.BlockSpec((1, d, tn), lambda i, j: (i, 0, j)),
            pl.BlockSpec((1, 1, tn), lambda i, j: (i, 0, j)),
        ],
        out_specs=pl.BlockSpec((1, mb, tn), lambda i, j: (i, 0, j)),
        out_shape=jax.ShapeDtypeStruct((depth, mb, n), F32),
        compiler_params=_cparams(("arbitrary", "arbitrary")),
        name="adaln_mod",
    )(cvec, w_mod, b_mod.reshape(depth, 1, n))


def _rope_store(t, cs, sn, out_ref):
    lane = lax.broadcasted_iota(jnp.int32, (1, LANES), 1)
    first = (lane % HEAD_DIM) < (HEAD_DIM // 2)
    for g in range(ATTN_W // LANES):
        tg = t[:, g * LANES:(g + 1) * LANES]
        sw = jnp.where(first, pltpu.roll(tg, LANES - HEAD_DIM // 2, 1),
                       pltpu.roll(tg, HEAD_DIM // 2, 1))
        out_ref[0, :, g * LANES:(g + 1) * LANES] = (tg * cs + sw * sn).astype(out_ref.dtype)


def _in_kernel(h_ref, mb_ref, mc_ref, w_ref, cs_ref, sn_ref,
               k_ref, v_ref, xr_ref, q_ref, y_ref, *, tm, n_ctx):
    j = pl.program_id(1)
    row = j * tm + lax.broadcasted_iota(jnp.int32, (tm, 1), 0)
    is_ctx = row < n_ctx
    shift = _row_mods(mb_ref, mc_ref, 0, is_ctx)
    scale = _row_mods(mb_ref, mc_ref, 1, is_ctx)
    x = (h_ref[0] * (1.0 + scale) + shift).astype(BF16)
    cs = cs_ref[...]
    sn = sn_ref[...]
    w = ATTN_W

    def proj(part):
        return jnp.dot(x, w_ref[:, part * w:(part + 1) * w], preferred_element_type=F32)

    _rope_store(proj(0), cs, sn, k_ref)
    v_ref[0] = proj(1).astype(v_ref.dtype)
    xr_ref[0] = proj(2)
    _rope_store(proj(3), cs, sn, q_ref)
    y_ref[0] = proj(4)


def _in_proj(h, mod, w_in, cs_tab, sn_tab, n_ctx, tm):
    b, t, d = h.shape
    n = w_in.shape[1]
    w = ATTN_W
    assert n == 5 * w
    row_spec = lambda width: pl.BlockSpec((1, tm, width), lambda i, j: (i, j, 0))
    return pl.pallas_call(
        functools.partial(_in_kernel, tm=tm, n_ctx=n_ctx),
        grid=(b, t // tm),
        in_specs=[
            row_spec(d),
            pl.BlockSpec((1, 6, d), lambda i, j: (i, 0, 0)),
            pl.BlockSpec((1, 6, d), lambda i, j: (b, 0, 0)),
            pl.BlockSpec((d, n), lambda i, j: (0, 0)),
            pl.BlockSpec((tm, LANES), lambda i, j: (j, 0)),
            pl.BlockSpec((tm, LANES), lambda i, j: (j, 0)),
        ],
        out_specs=[row_spec(w)] * 5,
        out_shape=[
            jax.ShapeDtypeStruct((b, t, w), BF16),
            jax.ShapeDtypeStruct((b, t, w), BF16),
            jax.ShapeDtypeStruct((b, t, w), F32),
            jax.ShapeDtypeStruct((b, t, w), BF16),
            jax.ShapeDtypeStruct((b, t, w), F32),
        ],
        compiler_params=_cparams(("arbitrary", "arbitrary")),
        name="in_proj",
    )(h, mod, mod, w_in, cs_tab, sn_tab)


def _attn_kernel(q_ref, k_ref, v_ref, lamv_ref, g_ref, o_ref, *, n_ctx, n_all, lam_init):
    j = pl.program_id(1)
    lv = lamv_ref[...]
    lam = (jnp.exp(jnp.sum(lv[0:1] * lv[1:2], keepdims=True))
           - jnp.exp(jnp.sum(lv[2:3] * lv[3:4], keepdims=True)) + lam_init)
    gain = g_ref[...] * (1.0 - lam_init)
    lane = lax.broadcasted_iota(jnp.int32, (1, HEAD_W), 1)
    dn = (((1,), (1,)), ((), ()))

    def softmax_parts(s):
        m = jnp.max(s, -1, keepdims=True)
        p = jnp.exp(s - m)
        return p, 1.0 / jnp.sum(p, -1, keepdims=True)

    def run(nk):
        for h in range(ATTN_HEADS):
            sl = slice(h * HEAD_W, (h + 1) * HEAD_W)
            qh = q_ref[0, :, sl]
            kh = k_ref[0, 0:nk, sl]
            vh = v_ref[0, 0:nk, sl]
            q1 = jnp.where(lane < HEAD_DIM, qh, jnp.zeros_like(qh))
            q2 = jnp.where(lane >= HEAD_DIM, qh, jnp.zeros_like(qh))
            p1, r1 = softmax_parts(lax.dot_general(q1, kh, dn, preferred_element_type=F32))
            p2, r2 = softmax_parts(lax.dot_general(q2, kh, dn, preferred_element_type=F32))
            a = p1 * r1 - p2 * (lam * r2)
            o = jnp.dot(a.astype(BF16), vh, preferred_element_type=F32)
            o = o * lax.rsqrt(jnp.mean(o * o, -1, keepdims=True) + RMS_EPS) * gain
            o_ref[0, :, sl] = o.astype(o_ref.dtype)

    @pl.when(j == 0)
    def _():
        run(n_ctx)

    @pl.when(j > 0)
    def _():
        run(n_all)


def _attention(q, k, v, lamv, gain, n_ctx, lam_init):
    b, t, w = q.shape
    tq = n_ctx
    return pl.pallas_call(
        functools.partial(_attn_kernel, n_ctx=n_ctx, n_all=t, lam_init=lam_init),
        grid=(b, t // tq),
        in_specs=[
            pl.BlockSpec((1, tq, w), lambda i, j: (i, j, 0)),
            pl.BlockSpec((1, t, w), lambda i, j: (i, 0, 0)),
            pl.BlockSpec((1, t, w), lambda i, j: (i, 0, 0)),
            pl.BlockSpec((4, HEAD_DIM), lambda i, j: (0, 0)),
            pl.BlockSpec((1, HEAD_W), lambda i, j: (0, 0)),
        ],
        out_specs=pl.BlockSpec((1, tq, w), lambda i, j: (i, j, 0)),
        out_shape=jax.ShapeDtypeStruct((b, t, w), BF16),
        compiler_params=_cparams(("arbitrary", "arbitrary")),
        name="diff_attn",
    )(q, k, v, lamv, gain)


def _gelu_tanh(x):
    return 0.5 * x * (1.0 + jnp.tanh(math.sqrt(2.0 / math.pi) * (x + 0.044715 * x * x * x)))


def _softplus(z):
    return jnp.maximum(z, 0.0) + jnp.log1p(jnp.exp(-jnp.abs(z)))


def _rg_kernel(xr_ref, y_ref, cw_ref, cb_ref, wg_ref, bias_ref, lam_ref, o_ref,
               xc_sc, acc_sc, a_sc, b_sc, h_sc, *, n_ctx, n_all, chunk):
    t, c, r = n_all, n_ctx, chunk
    half = MXU_DIM
    rgw = xc_sc.shape[1]

    row = lax.broadcasted_iota(jnp.int32, (t, 1), 0)
    in_ctx = row < c
    tl = jnp.where(in_ctx, row, row - c)
    seg = jnp.where(in_ctx, c, t - c)
    for g in range(rgw // LANES):
        sl = slice(g * LANES, (g + 1) * LANES)
        xg = xr_ref[0, :, sl]
        wv = cw_ref[:, sl]
        acc = cb_ref[:, sl] + xg * wv[2:3]
        acc = acc + jnp.where(tl >= 2, pltpu.roll(xg, 2, 0), 0.0) * wv[0:1]
        acc = acc + jnp.where(tl >= 1, pltpu.roll(xg, 1, 0), 0.0) * wv[1:2]
        acc = acc + jnp.where(tl + 1 < seg, pltpu.roll(xg, t - 1, 0), 0.0) * wv[3:4]
        xc_sc[:, sl] = acc

    n_chunks = t // r
    ctx_chunks = c // r
    for d in range(2):
        sp = _softplus(-lam_ref[d:d + 1, :])
        ba = bias_ref[2 * d:2 * d + 1, :]
        bx = bias_ref[2 * d + 1:2 * d + 2, :]

        def chunk_step(s, hc, d=d, sp=sp, ba=ba, bx=bx):
            if d == 0:
                ci = s
            else:
                ci = jnp.where(s < ctx_chunks, ctx_chunks - 1 - s,
                               n_chunks - 1 - (s - ctx_chunks))
            r0 = pl.multiple_of(ci * r, r)
            xc = xc_sc[pl.ds(r0, r), :]
            xb = xc.astype(BF16)
            for hh in range(rgw // half):
                hs = slice(hh * half, (hh + 1) * half)
                g2 = jnp.dot(xb[:, hs], wg_ref[d, hh], preferred_element_type=F32)
                rr = jax.nn.sigmoid(g2[:, :half] + ba[:, hs])
                gi = jax.nn.sigmoid(g2[:, half:] + bx[:, hs])
                log_a = (-RG_C) * rr * sp[:, hs]
                a_sc[:, hs] = jnp.exp(log_a)
                th = jnp.tanh(log_a)
                b_sc[:, hs] = jnp.sqrt(-2.0 * th / (1.0 - th)) * (gi * xc[:, hs])

            def row_step(i, hv):
                ri = i if d == 0 else r - 1 - i
                hv = a_sc[pl.ds(ri, 1), :] * hv + b_sc[pl.ds(ri, 1), :]
                h_sc[pl.ds(ri, 1), :] = hv
                return hv

            hc = lax.fori_loop(0, r, row_step, hc, unroll=8)
            if d == 0:
                acc_sc[pl.ds(r0, r), :] = h_sc[...]
            else:
                tot = acc_sc[pl.ds(r0, r), :] + h_sc[...]
                o_ref[0, pl.ds(r0, r), :] = (
                    tot * _gelu_tanh(y_ref[0, pl.ds(r0, r), :])).astype(o_ref.dtype)
            return hc

        lax.fori_loop(0, n_chunks, chunk_step, jnp.zeros((1, rgw), F32))


def _rglru(xr, y, conv_w, conv_b, wg, bias, lam, n_ctx, chunk):
    b, t, w = xr.shape
    full = lambda a: pl.BlockSpec(a.shape, lambda i: (0,) * a.ndim)
    seq = pl.BlockSpec((1, t, w), lambda i: (i, 0, 0))
    return pl.pallas_call(
        functools.partial(_rg_kernel, n_ctx=n_ctx, n_all=t, chunk=chunk),
        grid=(b,),
        in_specs=[seq, seq, full(conv_w), full(conv_b), full(wg), full(bias), full(lam)],
        out_specs=seq,
        out_shape=jax.ShapeDtypeStruct((b, t, w), BF16),
        scratch_shapes=[
            pltpu.VMEM((t, w), F32),
            pltpu.VMEM((t, w), F32),
            pltpu.VMEM((chunk, w), F32),
            pltpu.VMEM((chunk, w), F32),
            pltpu.VMEM((chunk, w), F32),
        ],
        compiler_params=_cparams(("arbitrary",)),
        name="rglru",
    )(xr, y, conv_w, conv_b, wg, bias, lam)


def _out_kernel(*refs, tm, n_ctx, alpha, moe):
    if moe:
        (att_ref, rg_ref, h_ref, mb_ref, mc_ref, w_ref, g_ref, b_ref, rt_ref,
         h1_ref, u_ref, route_ref) = refs
    else:
        (att_ref, rg_ref, h_ref, mb_ref, mc_ref, w_ref, g_ref, b_ref,
         h1_ref, u_ref) = refs
    j = pl.program_id(1)
    row = j * tm + lax.broadcasted_iota(jnp.int32, (tm, 1), 0)
    is_ctx = row < n_ctx
    aw = att_ref.shape[2]
    mix = (jnp.dot(att_ref[0], w_ref[0:aw, :], preferred_element_type=F32)
           + jnp.dot(rg_ref[0], w_ref[aw:, :], preferred_element_type=F32))
    g1 = _row_mods(mb_ref, mc_ref, 2, is_ctx)
    h1 = _layer_norm(alpha * h_ref[0] + (1.0 + g1) * mix, g_ref[...], b_ref[...])
    h1_ref[0] = h1
    sh2 = _row_mods(mb_ref, mc_ref, 3, is_ctx)
    sc2 = _row_mods(mb_ref, mc_ref, 4, is_ctx)
    u = h1 * (1.0 + sc2) + sh2
    u_ref[0] = u
    if moe:
        n_exp = route_ref.shape[1]
        logits = jnp.dot(u, rt_ref[...], preferred_element_type=F32,
                         precision=lax.Precision.HIGHEST)
        lane = lax.broadcasted_iota(jnp.int32, (1, LANES), 1)
        neg = -jnp.inf
        lg = jnp.where(lane < n_exp, logits, neg)
        m1 = jnp.max(lg, -1, keepdims=True)
        i1 = jnp.min(jnp.where(lg == m1, lane, LANES), -1, keepdims=True)
        lg2 = jnp.where(lane == i1, neg, lg)
        m2 = jnp.max(lg2, -1, keepdims=True)
        i2 = jnp.min(jnp.where(lg2 == m2, lane, LANES), -1, keepdims=True)
        e2 = jnp.exp(m2 - m1)
        den = 1.0 + e2
        route = jnp.where(lane == 0, i1.astype(F32),
                          jnp.where(lane == 1, i2.astype(F32),
                                    jnp.where(lane == 2, 1.0 / den,
                                              jnp.where(lane == 3, e2 / den, 0.0))))
        route_ref[...] = route[:, :n_exp]


def _out_proj(att, rg, h, mod, w_out, ln_g, ln_b, router, n_ctx, tm, alpha):
    b, t, d = h.shape
    aw = att.shape[2]
    moe = router is not None
    row_spec = lambda width: pl.BlockSpec((1, tm, width), lambda i, j: (i, j, 0))
    vec = pl.BlockSpec((1, d), lambda i, j: (0, 0))
    in_specs = [
        row_spec(aw), row_spec(aw), row_spec(d),
        pl.BlockSpec((1, 6, d), lambda i, j: (i, 0, 0)),
        pl.BlockSpec((1, 6, d), lambda i, j: (b, 0, 0)),
        pl.BlockSpec((d, d), lambda i, j: (0, 0)),
        vec, vec,
    ]
    args = [att, rg, h, mod, mod, w_out, ln_g, ln_b]
    out_specs = [row_spec(d), row_spec(d)]
    out_shape = [jax.ShapeDtypeStruct((b, t, d), F32), jax.ShapeDtypeStruct((b, t, d), F32)]
    if moe:
        n_exp = router.shape[1]
        rt = jnp.zeros((d, LANES), F32).at[:, :n_exp].set(router)
        in_specs.append(pl.BlockSpec((d, LANES), lambda i, j: (0, 0)))
        args.append(rt)
        tiles = t // tm
        out_specs.append(pl.BlockSpec((tm, n_exp), lambda i, j: (i * tiles + j, 0)))
        out_shape.append(jax.ShapeDtypeStruct((b * t, n_exp), F32))
    return pl.pallas_call(
        functools.partial(_out_kernel, tm=tm, n_ctx=n_ctx, alpha=alpha, moe=moe),
        grid=(b, t // tm),
        in_specs=in_specs,
        out_specs=out_specs,
        out_shape=out_shape,
        compiler_params=_cparams(("arbitrary", "arbitrary")),
        name="out_proj_moe" if moe else "out_proj",
    )(*args)


def _ffn_kernel(te_ref, nu_ref, x_ref, w1_ref, w3_ref, w2_ref, o_ref, *, fc):
    i = pl.program_id(0)

    @pl.when(i < nu_ref[0])
    def _():
        x = x_ref[...].astype(BF16)
        f = w1_ref.shape[2]
        acc = None
        for c in range(f // fc):
            cs = slice(c * fc, (c + 1) * fc)
            a = jnp.dot(x, w1_ref[0, :, cs], preferred_element_type=F32)
            bgate = jnp.dot(x, w3_ref[0, :, cs], preferred_element_type=F32)
            gact = (a * jax.nn.sigmoid(a) * bgate).astype(BF16)
            part = jnp.dot(gact, w2_ref[0, cs, :], preferred_element_type=F32)
            acc = part if acc is None else acc + part
        o_ref[...] = acc

    @pl.when(i >= nu_ref[0])
    def _():
        o_ref[...] = jnp.zeros_like(o_ref)


def _grouped_ffn(xs, w1, w3, w2, tile_expert, n_used, tm):
    rows, d = xs.shape
    f = w1.shape[2]
    fc = MXU_DIM if f % MXU_DIM == 0 else f
    n_tiles = rows // tm
    grid_spec = pltpu.PrefetchScalarGridSpec(
        num_scalar_prefetch=2,
        grid=(n_tiles,),
        in_specs=[
            pl.BlockSpec((tm, d), lambda i, te, nu: (i, 0)),
            pl.BlockSpec((1, d, f), lambda i, te, nu: (te[i], 0, 0)),
            pl.BlockSpec((1, d, f), lambda i, te, nu: (te[i], 0, 0)),
            pl.BlockSpec((1, f, d), lambda i, te, nu: (te[i], 0, 0)),
        ],
        out_specs=pl.BlockSpec((tm, d), lambda i, te, nu: (i, 0)),
    )
    return pl.pallas_call(
        functools.partial(_ffn_kernel, fc=fc),
        grid_spec=grid_spec,
        out_shape=jax.ShapeDtypeStruct((rows, d), F32),
        compiler_params=_cparams(("arbitrary",)),
        name="grouped_ffn",
    )(tile_expert, n_used, xs, w1, w3, w2)


def _dispatch_kernel(dst_ref, u_ref, init_ref, xs_ref, sem, *, ts):
    del init_ref
    i = pl.program_id(0)

    def row_copy(r, k):
        return pltpu.make_async_copy(
            u_ref.at[pl.ds(i * ts + r, 1)], xs_ref.at[pl.ds(dst_ref[0, 0, N_TOP * r + k], 1)], sem)

    def issue(r, carry):
        for k in range(N_TOP):
            row_copy(r, k).start()
        return carry

    def drain(r, carry):
        for k in range(N_TOP):
            row_copy(r, k).wait()
        return carry

    lax.fori_loop(0, ts, issue, 0)
    lax.fori_loop(0, ts, drain, 0)


def _dispatch(u, dst, rows, ts):
    n, d = u.shape
    steps = n // ts
    return pl.pallas_call(
        functools.partial(_dispatch_kernel, ts=ts),
        grid=(steps,),
        in_specs=[
            pl.BlockSpec((1, 1, N_TOP * ts), lambda i: (i, 0, 0), memory_space=pltpu.SMEM),
            pl.BlockSpec(memory_space=pl.ANY),
            pl.BlockSpec(memory_space=pl.ANY),
        ],
        out_specs=pl.BlockSpec(memory_space=pl.ANY),
        out_shape=jax.ShapeDtypeStruct((rows, d), F32),
        scratch_shapes=[pltpu.SemaphoreType.DMA(())],
        input_output_aliases={2: 0},
        compiler_params=_cparams(("arbitrary",)),
        name="moe_dispatch",
    )(dst.reshape(steps, 1, N_TOP * ts), u, jnp.zeros((rows, d), F32))


def _mix_epilogue(mixed, h1_ref, mb_ref, mc_ref, g_ref, b_ref, o_ref, row0, n_ctx, alpha):
    tc = mixed.shape[0]
    row = row0 + lax.broadcasted_iota(jnp.int32, (tc, 1), 0)
    g2 = _row_mods(mb_ref, mc_ref, 5, row < n_ctx)
    o_ref[0] = _layer_norm(alpha * h1_ref[0] + (1.0 + g2) * mixed, g_ref[...], b_ref[...])


def _dense_fin_kernel(y_ref, h1_ref, mb_ref, mc_ref, g_ref, b_ref, o_ref, *, tc, n_ctx, alpha):
    _mix_epilogue(y_ref[0], h1_ref, mb_ref, mc_ref, g_ref, b_ref, o_ref,
                  pl.program_id(1) * tc, n_ctx, alpha)


def _combine_kernel(dst_ref, ys_ref, route_ref, h1_ref, mb_ref, mc_ref, g_ref, b_ref, o_ref,
                    buf, sem, *, tc, n_ctx, alpha):
    def row_copy(r, k):
        return pltpu.make_async_copy(
            ys_ref.at[pl.ds(dst_ref[0, 0, N_TOP * r + k], 1)], buf.at[k, pl.ds(r, 1)], sem)

    def issue(r, carry):
        for k in range(N_TOP):
            row_copy(r, k).start()
        return carry

    def drain(r, carry):
        for k in range(N_TOP):
            row_copy(r, k).wait()
        return carry

    lax.fori_loop(0, tc, issue, 0)
    lax.fori_loop(0, tc, drain, 0)
    rt = route_ref[...]
    mixed = rt[:, 2:3] * buf[0] + rt[:, 3:4] * buf[1]
    _mix_epilogue(mixed, h1_ref, mb_ref, mc_ref, g_ref, b_ref, o_ref,
                  pl.program_id(1) * tc, n_ctx, alpha)


def _mix_finish(h1, mod, ln_g, ln_b, n_ctx, tc, alpha, y=None, ys=None, dst=None, route=None):
    b, t, d = h1.shape
    tiles = t // tc
    row_spec = pl.BlockSpec((1, tc, d), lambda i, j: (i, j, 0))
    vec = pl.BlockSpec((1, d), lambda i, j: (0, 0))
    common_specs = [
        row_spec,
        pl.BlockSpec((1, 6, d), lambda i, j: (i, 0, 0)),
        pl.BlockSpec((1, 6, d), lambda i, j: (b, 0, 0)),
        vec, vec,
    ]
    common_args = [h1, mod, mod, ln_g, ln_b]
    out_shape = jax.ShapeDtypeStruct((b, t, d), F32)
    if y is not None:
        return pl.pallas_call(
            functools.partial(_dense_fin_kernel, tc=tc, n_ctx=n_ctx, alpha=alpha),
            grid=(b, tiles),
            in_specs=[row_spec] + common_specs,
            out_specs=row_spec,
            out_shape=out_shape,
            compiler_params=_cparams(("arbitrary", "arbitrary")),
            name="dense_finish",
        )(y, *common_args)
    n_exp = route.shape[1]
    return pl.pallas_call(
        functools.partial(_combine_kernel, tc=tc, n_ctx=n_ctx, alpha=alpha),
        grid=(b, tiles),
        in_specs=[
            pl.BlockSpec((1, 1, N_TOP * tc), lambda i, j: (i * tiles + j, 0, 0),
                         memory_space=pltpu.SMEM),
            pl.BlockSpec(memory_space=pl.ANY),
            pl.BlockSpec((tc, n_exp), lambda i, j: (i * tiles + j, 0)),
        ] + common_specs,
        out_specs=row_spec,
        out_shape=out_shape,
        scratch_shapes=[pltpu.VMEM((N_TOP, tc, d), F32), pltpu.SemaphoreType.DMA(())],
        compiler_params=_cparams(("arbitrary", "arbitrary")),
        name="moe_combine",
    )(dst.reshape(b * tiles, 1, N_TOP * tc), ys, route, *common_args)


def _rope_tables(n_ctx, seq):
    rows = seq // GRID_W
    pairs = HEAD_DIM // 4
    rpos = jnp.repeat(jnp.arange(rows, dtype=F32), GRID_W)
    cpos = jnp.tile(jnp.arange(GRID_W, dtype=F32), rows)
    inv_freq = ROPE_BASE ** (-jnp.arange(pairs, dtype=F32) / pairs)
    ang = jnp.concatenate([rpos[:, None] * inv_freq, cpos[:, None] * inv_freq], -1)
    cos, sin = jnp.cos(ang), jnp.sin(ang)
    reps = LANES // HEAD_DIM
    cs = jnp.tile(jnp.concatenate([cos, cos], -1), (1, reps))
    sn = jnp.tile(jnp.concatenate([-sin, sin], -1), (1, reps))
    cs = jnp.concatenate([jnp.ones((n_ctx, LANES), F32), cs], 0)
    sn = jnp.concatenate([jnp.zeros((n_ctx, LANES), F32), sn], 0)
    return cs, sn


def _prep_w_in(w_in):
    n = w_in.shape[-1]
    half = HEAD_DIM // 2
    within = jnp.concatenate([2 * jnp.arange(half), 2 * jnp.arange(half) + 1])
    perm = jnp.arange(n)
    scale = jnp.ones((n,), F32)
    q0 = 3 * ATTN_W
    for base in (0, q0):
        blk = (base + HEAD_DIM * jnp.arange(ATTN_W // HEAD_DIM)[:, None] + within[None, :]).reshape(-1)
        perm = perm.at[base:base + ATTN_W].set(blk)
    scale = scale.at[q0:q0 + ATTN_W].set(HEAD_DIM ** -0.5)
    return (w_in[..., perm] * scale).astype(BF16)


def _prep_rg_gates(wa, wx):
    nd, g, bw, _ = wa.shape
    per = MXU_DIM // bw
    halves = g // per

    def bdiag(w):
        w = w.reshape(nd, halves, per, bw, bw)
        eye = jnp.eye(per, dtype=w.dtype)
        full = w[:, :, :, :, None, :] * eye[None, None, :, None, :, None]
        return full.reshape(nd, halves, per * bw, per * bw)

    return jnp.concatenate([bdiag(wa), bdiag(wx)], -1).astype(BF16)


def _routing_tables(route, n_exp, tm):
    e = route[:, :N_TOP].astype(jnp.int32).reshape(-1)
    oh = (e[:, None] == jnp.arange(n_exp)[None, :]).astype(jnp.int32)
    csum = jnp.cumsum(oh, axis=0)
    rank = jnp.sum((csum - oh) * oh, axis=1)
    counts = csum[-1]
    tiles_per = (counts + tm - 1) // tm
    tile_end = jnp.cumsum(tiles_per)
    off = (tile_end - tiles_per) * tm
    dst = off[e] + rank
    n_tiles = e.shape[0] // tm + n_exp
    tile_expert = jnp.searchsorted(tile_end, jnp.arange(n_tiles), side="right")
    tile_expert = jnp.minimum(tile_expert, n_exp - 1).astype(jnp.int32)
    return dst.astype(jnp.int32), tile_expert, tile_end[-1:].astype(jnp.int32), n_tiles * tm


def kernel(x, c, ctx, c_ctx, w_mod, b_mod, w_in, lam_q1, lam_k1, lam_q2, lam_k2, subln_g,
           conv_w, conv_b, rg_wa, rg_ba, rg_wx, rg_bx, rg_lambda, w_out,
           ln1_g, ln1_b, ln2_g, ln2_b, ffn_w1, ffn_w3, ffn_w2,
           moe_router, moe_w1, moe_w3, moe_w2):
    b, seq, d = x.shape
    n_ctx = ctx.shape[1]
    t = n_ctx + seq
    depth = w_in.shape[0]
    alpha = (2 * depth) ** 0.25
    assert seq % GRID_W == 0 and d - ATTN_W == ATTN_W and t % n_ctx == 0

    tm = _pick_tile(t, 768)
    tf = _pick_tile(t, 768)
    te = 512 if (N_TOP * b * t) % 512 == 0 else _pick_tile(N_TOP * b * t, 512)
    tc = _pick_tile(t, 256)
    ts = _pick_tile(b * t, 1024)
    chunk = math.gcd(n_ctx, 256)

    h = jnp.concatenate([ctx, x], axis=1)
    mrows = -(-(b + 1) // SUBLANES) * SUBLANES
    cvec = jnp.zeros((mrows, d), F32).at[:b].set(c).at[b].set(c_ctx)
    mod_all = _modulation(cvec, w_mod, b_mod).reshape(depth, mrows, 6, d)
    cs_tab, sn_tab = _rope_tables(n_ctx, seq)

    w_in_p = _prep_w_in(w_in)
    w_out_b = w_out.astype(BF16)
    lamv = jnp.stack([lam_q1, lam_k1, lam_q2, lam_k2], axis=1)
    rg_bias = jnp.stack([rg_ba, rg_bx], axis=2).reshape(depth, 4, -1)

    for i in range(depth):
        lam_init = 0.8 - 0.6 * math.exp(-0.3 * i)
        mod = mod_all[i]
        k, v, xr, q, y = _in_proj(h, mod, w_in_p[i], cs_tab, sn_tab, n_ctx, tm)
        att = _attention(q, k, v, lamv[i], subln_g[i][None, :], n_ctx, lam_init)
        rg = _rglru(xr, y, conv_w[i], conv_b[i][None, :], _prep_rg_gates(rg_wa[i], rg_wx[i]),
                    rg_bias[i], rg_lambda[i], n_ctx, chunk)
        j = i // 2
        router = moe_router[j] if i % 2 else None
        outs = _out_proj(att, rg, h, mod, w_out_b[i], ln1_g[i][None, :], ln1_b[i][None, :],
                         router, n_ctx, tm, alpha)
        h1, u = outs[0], outs[1]
        u2 = u.reshape(b * t, d)
        if i % 2 == 0:
            n_tiles = (b * t) // tf
            yd = _grouped_ffn(u2, ffn_w1[j][None].astype(BF16), ffn_w3[j][None].astype(BF16),
                              ffn_w2[j][None].astype(BF16), jnp.zeros((n_tiles,), jnp.int32),
                              jnp.full((1,), n_tiles, jnp.int32), tf)
            h = _mix_finish(h1, mod, ln2_g[i][None, :], ln2_b[i][None, :], n_ctx, tc, alpha,
                            y=yd.reshape(b, t, d))
        else:
            route = outs[2]
            n_exp = route.shape[1]
            dst, tile_expert, n_used, rows = _routing_tables(route, n_exp, te)
            xs = _dispatch(u2, dst, rows, ts)
            ys = _grouped_ffn(xs, moe_w1[j].astype(BF16), moe_w3[j].astype(BF16),
                              moe_w2[j].astype(BF16), tile_expert, n_used, te)
            h = _mix_finish(h1, mod, ln2_g[i][None, :], ln2_b[i][None, :], n_ctx, tc, alpha,
                            ys=ys, dst=dst, route=route)
    return h[:, n_ctx:, :]
```

```python
import functools
import math

import jax
import jax.numpy as jnp
from jax import lax
from jax.experimental import pallas as pl
from jax.experimental.pallas import tpu as pltpu

F32 = jnp.float32
BF16 = jnp.bfloat16

HEAD_DIM = 64
ATTN_HEADS = 4
HEAD_W = 2 * HEAD_DIM
ATTN_W = ATTN_HEADS * HEAD_W
GRID_W = 64
RG_BLOCKS = 8
RG_C = 8.0
CONV_W = 4
N_TOP = 2
ROPE_BASE = 10000.0
LN_EPS = 1e-5
RMS_EPS = 1e-5

LANES = 128
SUBLANES = 8
MXU_DIM = 256
VMEM_LIMIT = 56 * 1024 * 1024


def _cparams(sem, vmem=VMEM_LIMIT):
    return pltpu.CompilerParams(dimension_semantics=sem, vmem_limit_bytes=vmem)


def _pick_tile(n, pref):
    t = min(n, pref)
    while t > 8 and (n % t or t % 8):
        t -= 8
    assert n % t == 0
    return t


def _layer_norm(z, g, b):
    mu = jnp.mean(z, -1, keepdims=True)
    zc = z - mu
    var = jnp.mean(zc * zc, -1, keepdims=True)
    return zc * lax.rsqrt(var + LN_EPS) * g + b


def _row_mods(mb_ref, mc_ref, idx, is_ctx):
    return jnp.where(is_ctx, mc_ref[0, idx:idx + 1, :], mb_ref[0, idx:idx + 1, :])


def _sigmoid(x):
    return 0.5 * jnp.tanh(0.5 * x) + 0.5


def _mod_kernel(c_ref, w_ref, b_ref, o_ref):
    cv = c_ref[...]
    s = (cv * _sigmoid(cv)).astype(BF16)
    o_ref[0] = jnp.dot(s, w_ref[0].astype(BF16), preferred_element_type=F32) + b_ref[0]


def _modulation(cvec, w_mod, b_mod):
    depth, d, n = w_mod.shape
    mb = cvec.shape[0]
    tn = _pick_tile(n, 1536)
    return pl.pallas_call(
        _mod_kernel,
        grid=(depth, n // tn),
        in_specs=[
            pl.BlockSpec((mb, d), lambda i, j: (0, 0)),
            pl.BlockSpec((1, d, tn), lambda i, j: (i, 0, j)),
            pl.BlockSpec((1, 1, tn), lambda i, j: (i, 0, j)),
        ],
        out_specs=pl.BlockSpec((1, mb, tn), lambda i, j: (i, 0, j)),
        out_shape=jax.ShapeDtypeStruct((depth, mb, n), F32),
        compiler_params=_cparams(("arbitrary", "arbitrary")),
        name="adaln_mod",
    )(cvec, w_mod, b_mod.reshape(depth, 1, n))


def _rope_store(t, cs, sn, out_ref):
    lane = lax.broadcasted_iota(jnp.int32, (1, LANES), 1)
    first = (lane % HEAD_DIM) < (HEAD_DIM // 2)
    for g in range(ATTN_W // LANES):
        tg = t[:, g * LANES:(g + 1) * LANES]
        sw = jnp.where(first, pltpu.roll(tg, LANES - HEAD_DIM // 2, 1),
                       pltpu.roll(tg, HEAD_DIM // 2, 1))
        out_ref[0, :, g * LANES:(g + 1) * LANES] = (tg * cs + sw * sn).astype(out_ref.dtype)


def _in_kernel(h_ref, mb_ref, mc_ref, w_ref, cs_ref, sn_ref,
               k_ref, v_ref, xr_ref, q_ref, y_ref, *, tm, n_ctx):
    j = pl.program_id(1)
    row = j * tm + lax.broadcasted_iota(jnp.int32, (tm, 1), 0)
    is_ctx = row < n_ctx
    shift = _row_mods(mb_ref, mc_ref, 0, is_ctx)
    scale = _row_mods(mb_ref, mc_ref, 1, is_ctx)
    x = (h_ref[0] * (1.0 + scale) + shift).astype(BF16)
    cs = cs_ref[...]
    sn = sn_ref[...]
    w = ATTN_W

    def proj(part):
        return jnp.dot(x, w_ref[:, part * w:(part + 1) * w], preferred_element_type=F32)

    _rope_store(proj(0), cs, sn, k_ref)
    v_ref[0] = proj(1).astype(v_ref.dtype)
    xr_ref[0] = proj(2)
    _rope_store(proj(3), cs, sn, q_ref)
    y_ref[0] = proj(4)


def _in_proj(h, mod, w_in, cs_tab, sn_tab, n_ctx, tm):
    b, t, d = h.shape
    n = w_in.shape[1]
    w = ATTN_W
    assert n == 5 * w
    row_spec = lambda width: pl.BlockSpec((1, tm, width), lambda i, j: (i, j, 0))
    return pl.pallas_call(
        functools.partial(_in_kernel, tm=tm, n_ctx=n_ctx),
        grid=(b, t // tm),
        in_specs=[
            row_spec(d),
            pl.BlockSpec((1, 6, d), lambda i, j: (i, 0, 0)),
            pl.BlockSpec((1, 6, d), lambda i, j: (b, 0, 0)),
            pl.BlockSpec((d, n), lambda i, j: (0, 0)),
            pl.BlockSpec((tm, LANES), lambda i, j: (j, 0)),
            pl.BlockSpec((tm, LANES), lambda i, j: (j, 0)),
        ],
        out_specs=[row_spec(w)] * 5,
        out_shape=[
            jax.ShapeDtypeStruct((b, t, w), BF16),
            jax.ShapeDtypeStruct((b, t, w), BF16),
            jax.ShapeDtypeStruct((b, t, w), F32),
            jax.ShapeDtypeStruct((b, t, w), BF16),
            jax.ShapeDtypeStruct((b, t, w), F32),
        ],
        compiler_params=_cparams(("arbitrary", "arbitrary")),
        name="in_proj",
    )(h, mod, mod, w_in, cs_tab, sn_tab)


def _attn_kernel(q_ref, k_ref, v_ref, lamv_ref, g_ref, o_ref, *, n_ctx, n_all, lam_init):
    j = pl.program_id(1)
    lv = lamv_ref[...]
    lam = (jnp.exp(jnp.sum(lv[0:1] * lv[1:2], keepdims=True))
           - jnp.exp(jnp.sum(lv[2:3] * lv[3:4], keepdims=True)) + lam_init)
    gain = g_ref[...] * (1.0 - lam_init)
    lane = lax.broadcasted_iota(jnp.int32, (1, HEAD_W), 1)
    dn = (((1,), (1,)), ((), ()))

    def softmax_av(qm, kh, vext):
        s = lax.dot_general(qm, kh, dn, preferred_element_type=F32)
        p = jnp.exp2(s - jnp.max(s, -1, keepdims=True)).astype(BF16)
        oe = jnp.dot(p, vext, preferred_element_type=F32)
        return oe[:, :HEAD_W] / oe[:, HEAD_W:HEAD_W + 1]

    def run(nk):
        ones = jnp.ones((nk, HEAD_W), BF16)
        for h in range(ATTN_HEADS):
            sl = slice(h * HEAD_W, (h + 1) * HEAD_W)
            qh = q_ref[0, :, sl]
            kh = k_ref[0, 0:nk, sl]
            vext = jnp.concatenate([v_ref[0, 0:nk, sl], ones], axis=1)
            q1 = jnp.where(lane < HEAD_DIM, qh, jnp.zeros_like(qh))
            q2 = jnp.where(lane >= HEAD_DIM, qh, jnp.zeros_like(qh))
            o = softmax_av(q1, kh, vext) - lam * softmax_av(q2, kh, vext)
            o = o * lax.rsqrt(jnp.mean(o * o, -1, keepdims=True) + RMS_EPS) * gain
            o_ref[0, :, sl] = o.astype(o_ref.dtype)

    @pl.when(j == 0)
    def _():
        run(n_ctx)

    @pl.when(j > 0)
    def _():
        run(n_all)


def _attention(q, k, v, lamv, gain, n_ctx, lam_init):
    b, t, w = q.shape
    tq = n_ctx
    return pl.pallas_call(
        functools.partial(_attn_kernel, n_ctx=n_ctx, n_all=t, lam_init=lam_init),
        grid=(b, t // tq),
        in_specs=[
            pl.BlockSpec((1, tq, w), lambda i, j: (i, j, 0)),
            pl.BlockSpec((1, t, w), lambda i, j: (i, 0, 0)),
            pl.BlockSpec((1, t, w), lambda i, j: (i, 0, 0)),
            pl.BlockSpec((4, HEAD_DIM), lambda i, j: (0, 0)),
            pl.BlockSpec((1, HEAD_W), lambda i, j: (0, 0)),
        ],
        out_specs=pl.BlockSpec((1, tq, w), lambda i, j: (i, j, 0)),
        out_shape=jax.ShapeDtypeStruct((b, t, w), BF16),
        compiler_params=_cparams(("arbitrary", "arbitrary")),
        name="diff_attn",
    )(q, k, v, lamv, gain)


def _gelu_tanh(x):
    return 0.5 * x * (1.0 + jnp.tanh(math.sqrt(2.0 / math.pi) * (x + 0.044715 * x * x * x)))


def _softplus(z):
    return jnp.maximum(z, 0.0) + jnp.log1p(jnp.exp(-jnp.abs(z)))


def _rg_kernel(xr_ref, y_ref, cw_ref, cb_ref, wg_ref, bias_ref, lam_ref, o_ref,
               xc_sc, acc_sc, a_sl, b_sl, *, n_ctx, n_all, chunk):
    t, c, r = n_all, n_ctx, chunk
    half = MXU_DIM
    rgw = xc_sc.shape[1]
    groups = rgw // LANES
    nseg = SUBLANES
    seg = r // nseg

    row = lax.broadcasted_iota(jnp.int32, (t, 1), 0)
    in_ctx = row < c
    tl = jnp.where(in_ctx, row, row - c)
    span = jnp.where(in_ctx, c, t - c)
    for g in range(groups):
        sl = slice(g * LANES, (g + 1) * LANES)
        xg = xr_ref[0, :, sl]
        wv = cw_ref[:, sl]
        acc = cb_ref[:, sl] + xg * wv[2:3]
        acc = acc + jnp.where(tl >= 2, pltpu.roll(xg, 2, 0), 0.0) * wv[0:1]
        acc = acc + jnp.where(tl >= 1, pltpu.roll(xg, 1, 0), 0.0) * wv[1:2]
        acc = acc + jnp.where(tl + 1 < span, pltpu.roll(xg, t - 1, 0), 0.0) * wv[3:4]
        xc_sc[:, sl] = acc

    n_chunks = t // r
    ctx_chunks = c // r
    for d in range(2):
        sp = _softplus(-lam_ref[d:d + 1, :])
        ba = bias_ref[2 * d:2 * d + 1, :]
        bx = bias_ref[2 * d + 1:2 * d + 2, :]

        def chunk_step(s, hc, d=d, sp=sp, ba=ba, bx=bx):
            if d == 0:
                ci = s
            else:
                ci = jnp.where(s < ctx_chunks, ctx_chunks - 1 - s,
                               n_chunks - 1 - (s - ctx_chunks))
            r0 = pl.multiple_of(ci * r, r)
            xc = xc_sc[pl.ds(r0, r), :]
            xb = xc.astype(BF16)
            for hh in range(rgw // half):
                hs = slice(hh * half, (hh + 1) * half)
                g2 = jnp.dot(xb[:, hs], wg_ref[d, hh], preferred_element_type=F32)
                rr = _sigmoid(g2[:, :half] + ba[:, hs])
                gi = _sigmoid(g2[:, half:] + bx[:, hs])
                log_a = (-RG_C) * rr * sp[:, hs]
                a = jnp.exp(log_a)
                th = jnp.tanh(log_a)
                bt = jnp.sqrt(-2.0 * th / (1.0 - th)) * (gi * xc[:, hs])
                for gg in range(half // LANES):
                    g = hh * (half // LANES) + gg
                    ls = slice(gg * LANES, (gg + 1) * LANES)
                    for k in range(nseg):
                        rows = slice(k * seg, (k + 1) * seg)
                        a_sl[g, pl.ds(k, seg, stride=nseg), :] = a[rows, ls]
                        b_sl[g, pl.ds(k, seg, stride=nseg), :] = bt[rows, ls]

            def seg_step(ii, carry):
                i = ii if d == 0 else seg - 1 - ii
                base = pl.multiple_of(i * nseg, nseg)
                hs_, as_ = carry
                nh, na = [], []
                for g in range(groups):
                    av = a_sl[g, pl.ds(base, nseg), :]
                    hv = av * hs_[g] + b_sl[g, pl.ds(base, nseg), :]
                    pv = av * as_[g]
                    b_sl[g, pl.ds(base, nseg), :] = hv
                    a_sl[g, pl.ds(base, nseg), :] = pv
                    nh.append(hv)
                    na.append(pv)
                return tuple(nh), tuple(na)

            zero = jnp.zeros((nseg, LANES), F32)
            one = jnp.ones((nseg, LANES), F32)
            h_end, a_end = lax.fori_loop(
                0, seg, seg_step, ((zero,) * groups, (one,) * groups), unroll=4)

            order = range(nseg) if d == 0 else range(nseg - 1, -1, -1)
            new_carry = []
            for g in range(groups):
                ls = slice(g * LANES, (g + 1) * LANES)
                h_in = hc[:, ls]
                for k in order:
                    rows = pl.ds(r0 + k * seg, seg)
                    h_true = (b_sl[g, pl.ds(k, seg, stride=nseg), :]
                              + a_sl[g, pl.ds(k, seg, stride=nseg), :] * h_in)
                    if d == 0:
                        acc_sc[rows, ls] = h_true
                    else:
                        tot = acc_sc[rows, ls] + h_true
                        o_ref[0, rows, ls] = (tot * _gelu_tanh(y_ref[0, rows, ls])).astype(o_ref.dtype)
                    h_in = h_end[g][k:k + 1, :] + a_end[g][k:k + 1, :] * h_in
                new_carry.append(h_in)
            return jnp.concatenate(new_carry, axis=1)

        lax.fori_loop(0, n_chunks, chunk_step, jnp.zeros((1, rgw), F32))


def _rglru(xr, y, conv_w, conv_b, wg, bias, lam, n_ctx, chunk):
    b, t, w = xr.shape
    full = lambda a: pl.BlockSpec(a.shape, lambda i: (0,) * a.ndim)
    seq = pl.BlockSpec((1, t, w), lambda i: (i, 0, 0))
    return pl.pallas_call(
        functools.partial(_rg_kernel, n_ctx=n_ctx, n_all=t, chunk=chunk),
        grid=(b,),
        in_specs=[seq, seq, full(conv_w), full(conv_b), full(wg), full(bias), full(lam)],
        out_specs=seq,
        out_shape=jax.ShapeDtypeStruct((b, t, w), BF16),
        scratch_shapes=[
            pltpu.VMEM((t, w), F32),
            pltpu.VMEM((t, w), F32),
            pltpu.VMEM((w // LANES, chunk, LANES), F32),
            pltpu.VMEM((w // LANES, chunk, LANES), F32),
        ],
        compiler_params=_cparams(("arbitrary",)),
        name="rglru",
    )(xr, y, conv_w, conv_b, wg, bias, lam)


def _out_kernel(*refs, tm, n_ctx, alpha, moe):
    if moe:
        (att_ref, rg_ref, h_ref, mb_ref, mc_ref, w_ref, g_ref, b_ref, rt_ref,
         h1_ref, u_ref, route_ref) = refs
    else:
        att_ref, rg_ref, h_ref, mb_ref, mc_ref, w_ref, g_ref, b_ref, h1_ref = refs
    j = pl.program_id(1)
    row = j * tm + lax.broadcasted_iota(jnp.int32, (tm, 1), 0)
    is_ctx = row < n_ctx
    aw = att_ref.shape[2]
    mix = (jnp.dot(att_ref[0], w_ref[0:aw, :], preferred_element_type=F32)
           + jnp.dot(rg_ref[0], w_ref[aw:, :], preferred_element_type=F32))
    g1 = _row_mods(mb_ref, mc_ref, 2, is_ctx)
    h1 = _layer_norm(alpha * h_ref[0] + (1.0 + g1) * mix, g_ref[...], b_ref[...])
    h1_ref[0] = h1
    if not moe:
        return
    sh2 = _row_mods(mb_ref, mc_ref, 3, is_ctx)
    sc2 = _row_mods(mb_ref, mc_ref, 4, is_ctx)
    u = h1 * (1.0 + sc2) + sh2
    u_ref[0] = u
    n_exp = route_ref.shape[1]
    u_hi = u.astype(BF16)
    u_lo = (u - u_hi.astype(F32)).astype(BF16)
    prod = (jnp.dot(u_hi, rt_ref[...], preferred_element_type=F32)
            + jnp.dot(u_lo, rt_ref[...], preferred_element_type=F32))
    logits = prod[:, :LANES] + prod[:, LANES:]
    lane = lax.broadcasted_iota(jnp.int32, (1, LANES), 1)
    neg = -jnp.inf
    lg = jnp.where(lane < n_exp, logits, neg)
    m1 = jnp.max(lg, -1, keepdims=True)
    i1 = jnp.min(jnp.where(lg == m1, lane, LANES), -1, keepdims=True)
    lg2 = jnp.where(lane == i1, neg, lg)
    m2 = jnp.max(lg2, -1, keepdims=True)
    i2 = jnp.min(jnp.where(lg2 == m2, lane, LANES), -1, keepdims=True)
    e2 = jnp.exp(m2 - m1)
    den = 1.0 + e2
    route = jnp.where(lane == 0, i1.astype(F32),
                      jnp.where(lane == 1, i2.astype(F32),
                                jnp.where(lane == 2, 1.0 / den,
                                          jnp.where(lane == 3, e2 / den, 0.0))))
    route_ref[...] = route[:, :n_exp]


def _out_proj(att, rg, h, mod, w_out, ln_g, ln_b, router, n_ctx, tm, alpha):
    b, t, d = h.shape
    aw = att.shape[2]
    moe = router is not None
    row_spec = lambda width: pl.BlockSpec((1, tm, width), lambda i, j: (i, j, 0))
    vec = pl.BlockSpec((1, d), lambda i, j: (0, 0))
    in_specs = [
        row_spec(aw), row_spec(aw), row_spec(d),
        pl.BlockSpec((1, 6, d), lambda i, j: (i, 0, 0)),
        pl.BlockSpec((1, 6, d), lambda i, j: (b, 0, 0)),
        pl.BlockSpec((d, d), lambda i, j: (0, 0)),
        vec, vec,
    ]
    args = [att, rg, h, mod, mod, w_out, ln_g, ln_b]
    out_specs = [row_spec(d)]
    out_shape = [jax.ShapeDtypeStruct((b, t, d), F32)]
    if moe:
        n_exp = router.shape[1]
        r_hi = router.astype(BF16)
        r_lo = (router - r_hi.astype(F32)).astype(BF16)
        pad = lambda a: jnp.zeros((d, LANES), BF16).at[:, :n_exp].set(a)
        in_specs.append(pl.BlockSpec((d, 2 * LANES), lambda i, j: (0, 0)))
        args.append(jnp.concatenate([pad(r_hi), pad(r_lo)], axis=1))
        tiles = t // tm
        out_specs += [row_spec(d), pl.BlockSpec((tm, n_exp), lambda i, j: (i * tiles + j, 0))]
        out_shape += [jax.ShapeDtypeStruct((b, t, d), F32),
                      jax.ShapeDtypeStruct((b * t, n_exp), F32)]
    return pl.pallas_call(
        functools.partial(_out_kernel, tm=tm, n_ctx=n_ctx, alpha=alpha, moe=moe),
        grid=(b, t // tm),
        in_specs=in_specs,
        out_specs=out_specs,
        out_shape=out_shape,
        compiler_params=_cparams(("arbitrary", "arbitrary")),
        name="out_proj_moe" if moe else "out_proj",
    )(*args)


def _swiglu(x, w1_ref, w3_ref, w2_ref, fc):
    f = w1_ref.shape[2]
    acc = None
    for c in range(f // fc):
        cs = slice(c * fc, (c + 1) * fc)
        a = jnp.dot(x, w1_ref[0, :, cs], preferred_element_type=F32)
        bgate = jnp.dot(x, w3_ref[0, :, cs], preferred_element_type=F32)
        gact = (a * _sigmoid(a) * bgate).astype(BF16)
        part = jnp.dot(gact, w2_ref[0, cs, :], preferred_element_type=F32)
        acc = part if acc is None else acc + part
    return acc


def _ffn_chunk(f):
    return MXU_DIM if f % MXU_DIM == 0 else f


def _dense_ffn_kernel(h1_ref, mb_ref, mc_ref, w1_ref, w3_ref, w2_ref, g_ref, b_ref, o_ref,
                      *, tf, n_ctx, alpha, fc):
    row = pl.program_id(1) * tf + lax.broadcasted_iota(jnp.int32, (tf, 1), 0)
    is_ctx = row < n_ctx
    sh2 = _row_mods(mb_ref, mc_ref, 3, is_ctx)
    sc2 = _row_mods(mb_ref, mc_ref, 4, is_ctx)
    g2 = _row_mods(mb_ref, mc_ref, 5, is_ctx)
    h1 = h1_ref[0]
    u = (h1 * (1.0 + sc2) + sh2).astype(BF16)
    mixed = _swiglu(u, w1_ref, w3_ref, w2_ref, fc)
    o_ref[0] = _layer_norm(alpha * h1 + (1.0 + g2) * mixed, g_ref[...], b_ref[...])


def _dense_ffn(h1, mod, w1, w3, w2, ln_g, ln_b, n_ctx, tf, alpha):
    b, t, d = h1.shape
    f = w1.shape[2]
    row_spec = pl.BlockSpec((1, tf, d), lambda i, j: (i, j, 0))
    vec = pl.BlockSpec((1, d), lambda i, j: (0, 0))
    const3 = lambda a: pl.BlockSpec(a.shape, lambda i, j: (0, 0, 0))
    return pl.pallas_call(
        functools.partial(_dense_ffn_kernel, tf=tf, n_ctx=n_ctx, alpha=alpha, fc=_ffn_chunk(f)),
        grid=(b, t // tf),
        in_specs=[
            row_spec,
            pl.BlockSpec((1, 6, d), lambda i, j: (i, 0, 0)),
            pl.BlockSpec((1, 6, d), lambda i, j: (b, 0, 0)),
            const3(w1), const3(w3), const3(w2), vec, vec,
        ],
        out_specs=row_spec,
        out_shape=jax.ShapeDtypeStruct((b, t, d), F32),
        compiler_params=_cparams(("arbitrary", "arbitrary")),
        name="dense_ffn",
    )(h1, mod, mod, w1, w3, w2, ln_g, ln_b)


def _moe_ffn_kernel(te_ref, nu_ref, src_ref, nxt_ref, u_ref, w1_ref, w3_ref, w2_ref, o_ref,
                    xbuf, sem, *, tm, fc):
    del te_ref
    i = pl.program_id(0)
    n_used = nu_ref[0]
    slot = i % 2

    def gather(idx_ref, dst_slot):
        def issue(r, carry):
            pltpu.make_async_copy(u_ref.at[pl.ds(idx_ref[0, 0, r], 1)],
                                  xbuf.at[dst_slot, pl.ds(r, 1)], sem.at[dst_slot]).start()
            return carry
        lax.fori_loop(0, tm, issue, 0, unroll=8)

    @pl.when(jnp.logical_and(i == 0, n_used > 0))
    def _():
        gather(src_ref, 0)

    @pl.when(i + 1 < n_used)
    def _():
        gather(nxt_ref, 1 - slot)

    @pl.when(i < n_used)
    def _():
        pltpu.make_async_copy(u_ref.at[pl.ds(0, tm)], xbuf.at[slot], sem.at[slot]).wait()
        o_ref[...] = _swiglu(xbuf[slot].astype(BF16), w1_ref, w3_ref, w2_ref, fc)

    @pl.when(i >= n_used)
    def _():
        o_ref[...] = jnp.zeros_like(o_ref)


def _moe_ffn(u, src, w1, w3, w2, tile_expert, n_used, tm):
    n, d = u.shape
    f = w1.shape[2]
    n_tiles = src.shape[0] // tm
    src3 = src.reshape(n_tiles, 1, tm)
    wspec = lambda shape: pl.BlockSpec(shape, lambda i, te, nu: (te[i], 0, 0))
    grid_spec = pltpu.PrefetchScalarGridSpec(
        num_scalar_prefetch=2,
        grid=(n_tiles,),
        in_specs=[
            pl.BlockSpec((1, 1, tm), lambda i, te, nu: (i, 0, 0), memory_space=pltpu.SMEM),
            pl.BlockSpec((1, 1, tm), lambda i, te, nu: (jnp.minimum(i + 1, n_tiles - 1), 0, 0),
                         memory_space=pltpu.SMEM),
            pl.BlockSpec(memory_space=pl.ANY),
            wspec((1, d, f)), wspec((1, d, f)), wspec((1, f, d)),
        ],
        out_specs=pl.BlockSpec((tm, d), lambda i, te, nu: (i, 0)),
        scratch_shapes=[pltpu.VMEM((2, tm, d), F32), pltpu.SemaphoreType.DMA((2,))],
    )
    return pl.pallas_call(
        functools.partial(_moe_ffn_kernel, tm=tm, fc=_ffn_chunk(f)),
        grid_spec=grid_spec,
        out_shape=jax.ShapeDtypeStruct((n_tiles * tm, d), F32),
        compiler_params=_cparams(("arbitrary",)),
        name="moe_ffn",
    )(tile_expert, n_used, src3, src3, u, w1, w3, w2)


def _combine_kernel(dst_ref, nxt_ref, ys_ref, route_ref, h1_ref, mb_ref, mc_ref, g_ref, b_ref,
                    o_ref, buf, sem, *, tc, n_ctx, first_tile, alpha):
    i, j = pl.program_id(0), pl.program_id(1)
    nj = pl.num_programs(1)
    s = i * nj + j
    total = pl.num_programs(0) * nj
    slot = s % 2

    def gather(idx_ref, dslot):
        def issue(r, carry):
            for k in range(N_TOP):
                pltpu.make_async_copy(ys_ref.at[pl.ds(idx_ref[0, 0, N_TOP * r + k], 1)],
                                      buf.at[dslot, pl.ds(k * tc + r, 1)], sem.at[dslot]).start()
            return carry
        lax.fori_loop(0, tc, issue, 0, unroll=4)

    @pl.when(s == 0)
    def _():
        gather(dst_ref, 0)

    @pl.when(s + 1 < total)
    def _():
        gather(nxt_ref, 1 - slot)

    pltpu.make_async_copy(ys_ref.at[pl.ds(0, N_TOP * tc)], buf.at[slot], sem.at[slot]).wait()
    rt = route_ref[...]
    mixed = rt[:, 2:3] * buf[slot, 0:tc] + rt[:, 3:4] * buf[slot, tc:N_TOP * tc]
    row = (first_tile + j) * tc + lax.broadcasted_iota(jnp.int32, (tc, 1), 0)
    g2 = _row_mods(mb_ref, mc_ref, 5, row < n_ctx)
    o_ref[0] = _layer_norm(alpha * h1_ref[0] + (1.0 + g2) * mixed, g_ref[...], b_ref[...])


def _moe_combine(h1, mod, ln_g, ln_b, ys, dst, route, n_ctx, tc, alpha, skip_ctx):
    b, t, d = h1.shape
    tiles = t // tc
    off = n_ctx // tc if skip_ctx else 0
    nj = tiles - off
    n_exp = route.shape[1]
    dst3 = dst.reshape(b * tiles, 1, N_TOP * tc)

    def cur(i, j):
        return i * tiles + off + j

    def nxt(i, j):
        i2 = jnp.minimum(i + (j + 1) // nj, b - 1)
        return i2 * tiles + off + (j + 1) % nj

    vec = pl.BlockSpec((1, d), lambda i, j: (0, 0))
    return pl.pallas_call(
        functools.partial(_combine_kernel, tc=tc, n_ctx=n_ctx, first_tile=off, alpha=alpha),
        grid=(b, nj),
        in_specs=[
            pl.BlockSpec((1, 1, N_TOP * tc), lambda i, j: (cur(i, j), 0, 0),
                         memory_space=pltpu.SMEM),
            pl.BlockSpec((1, 1, N_TOP * tc), lambda i, j: (nxt(i, j), 0, 0),
                         memory_space=pltpu.SMEM),
            pl.BlockSpec(memory_space=pl.ANY),
            pl.BlockSpec((tc, n_exp), lambda i, j: (cur(i, j), 0)),
            pl.BlockSpec((1, tc, d), lambda i, j: (i, off + j, 0)),
            pl.BlockSpec((1, 6, d), lambda i, j: (i, 0, 0)),
            pl.BlockSpec((1, 6, d), lambda i, j: (b, 0, 0)),
            vec, vec,
        ],
        out_specs=pl.BlockSpec((1, tc, d), lambda i, j: (i, j, 0)),
        out_shape=jax.ShapeDtypeStruct((b, nj * tc, d), F32),
        scratch_shapes=[pltpu.VMEM((2, N_TOP * tc, d), F32), pltpu.SemaphoreType.DMA((2,))],
        compiler_params=_cparams(("arbitrary", "arbitrary")),
        name="moe_combine",
    )(dst3, dst3, ys, route, h1, mod, mod, ln_g, ln_b)


def _rope_tables(n_ctx, seq):
    rows = seq // GRID_W
    pairs = HEAD_DIM // 4
    rpos = jnp.repeat(jnp.arange(rows, dtype=F32), GRID_W)
    cpos = jnp.tile(jnp.arange(GRID_W, dtype=F32), rows)
    inv_freq = ROPE_BASE ** (-jnp.arange(pairs, dtype=F32) / pairs)
    ang = jnp.concatenate([rpos[:, None] * inv_freq, cpos[:, None] * inv_freq], -1)
    cos, sin = jnp.cos(ang), jnp.sin(ang)
    reps = LANES // HEAD_DIM
    cs = jnp.tile(jnp.concatenate([cos, cos], -1), (1, reps))
    sn = jnp.tile(jnp.concatenate([-sin, sin], -1), (1, reps))
    cs = jnp.concatenate([jnp.ones((n_ctx, LANES), F32), cs], 0)
    sn = jnp.concatenate([jnp.zeros((n_ctx, LANES), F32), sn], 0)
    return cs, sn


def _prep_w_in(w_in):
    n = w_in.shape[-1]
    half = HEAD_DIM // 2
    within = jnp.concatenate([2 * jnp.arange(half), 2 * jnp.arange(half) + 1])
    perm = jnp.arange(n)
    scale = jnp.ones((n,), F32)
    q0 = 3 * ATTN_W
    for base in (0, q0):
        blk = (base + HEAD_DIM * jnp.arange(ATTN_W // HEAD_DIM)[:, None] + within[None, :]).reshape(-1)
        perm = perm.at[base:base + ATTN_W].set(blk)
    scale = scale.at[q0:q0 + ATTN_W].set(HEAD_DIM ** -0.5 * math.log2(math.e))
    return (w_in[..., perm] * scale).astype(BF16)


def _prep_rg_gates(wa, wx):
    nd, g, bw, _ = wa.shape
    per = MXU_DIM // bw
    halves = g // per

    def bdiag(w):
        w = w.reshape(nd, halves, per, bw, bw)
        eye = jnp.eye(per, dtype=w.dtype)
        full = w[:, :, :, :, None, :] * eye[None, None, :, None, :, None]
        return full.reshape(nd, halves, per * bw, per * bw)

    return jnp.concatenate([bdiag(wa), bdiag(wx)], -1).astype(BF16)


def _routing_tables(route, keep, n_exp, tm):
    e = route[:, :N_TOP].astype(jnp.int32).reshape(-1)
    keep2 = jnp.repeat(keep, N_TOP)
    oh = ((e[:, None] == jnp.arange(n_exp)[None, :]) & keep2[:, None]).astype(jnp.int32)
    csum = jnp.cumsum(oh, axis=0)
    rank = jnp.sum((csum - oh) * oh, axis=1)
    counts = csum[-1]
    tiles_per = (counts + tm - 1) // tm
    tile_end = jnp.cumsum(tiles_per)
    off = (tile_end - tiles_per) * tm
    n_tiles = e.shape[0] // tm + n_exp
    rows = n_tiles * tm
    dst = jnp.where(keep2, jnp.sum(oh * off[None, :], axis=1) + rank, rows).astype(jnp.int32)
    src = jnp.zeros((rows,), jnp.int32).at[dst].set(
        jnp.arange(e.shape[0], dtype=jnp.int32) // N_TOP, mode="drop")
    tile_ids = jnp.arange(n_tiles)
    tile_expert = jnp.sum((tile_ids[:, None] >= tile_end[None, :]).astype(jnp.int32), axis=1)
    tile_expert = jnp.minimum(tile_expert, n_exp - 1).astype(jnp.int32)
    return dst, src, tile_expert, tile_end[-1:].astype(jnp.int32)


def kernel(x, c, ctx, c_ctx, w_mod, b_mod, w_in, lam_q1, lam_k1, lam_q2, lam_k2, subln_g,
           conv_w, conv_b, rg_wa, rg_ba, rg_wx, rg_bx, rg_lambda, w_out,
           ln1_g, ln1_b, ln2_g, ln2_b, ffn_w1, ffn_w3, ffn_w2,
           moe_router, moe_w1, moe_w3, moe_w2):
    b, seq, d = x.shape
    n_ctx = ctx.shape[1]
    t = n_ctx + seq
    depth = w_in.shape[0]
    alpha = (2 * depth) ** 0.25
    assert seq % GRID_W == 0 and d - ATTN_W == ATTN_W and t % n_ctx == 0

    tm = _pick_tile(t, 768)
    tf = _pick_tile(t, 768)
    te = 512 if (N_TOP * b * t) % 512 == 0 else _pick_tile(N_TOP * b * t, 512)
    tc = _pick_tile(n_ctx, 256)
    chunk = math.gcd(n_ctx, 256)

    h = jnp.concatenate([ctx, x], axis=1)
    mrows = -(-(b + 1) // SUBLANES) * SUBLANES
    cvec = jnp.zeros((mrows, d), F32).at[:b].set(c).at[b].set(c_ctx)
    mod_all = _modulation(cvec, w_mod, b_mod).reshape(depth, mrows, 6, d)
    cs_tab, sn_tab = _rope_tables(n_ctx, seq)

    w_in_p = _prep_w_in(w_in)
    w_out_b = w_out.astype(BF16)
    lamv = jnp.stack([lam_q1, lam_k1, lam_q2, lam_k2], axis=1)
    rg_bias = jnp.stack([rg_ba, rg_bx], axis=2).reshape(depth, 4, -1)
    is_lat = jnp.tile(jnp.arange(t) >= n_ctx, b)

    for i in range(depth):
        last = i == depth - 1
        lam_init = 0.8 - 0.6 * math.exp(-0.3 * i)
        mod = mod_all[i]
        g2_ln = (ln2_g[i][None, :], ln2_b[i][None, :])
        k, v, xr, q, y = _in_proj(h, mod, w_in_p[i], cs_tab, sn_tab, n_ctx, tm)
        att = _attention(q, k, v, lamv[i], subln_g[i][None, :], n_ctx, lam_init)
        rg = _rglru(xr, y, conv_w[i], conv_b[i][None, :], _prep_rg_gates(rg_wa[i], rg_wx[i]),
                    rg_bias[i], rg_lambda[i], n_ctx, chunk)
        j = i // 2
        router = moe_router[j] if i % 2 else None
        outs = _out_proj(att, rg, h, mod, w_out_b[i], ln1_g[i][None, :], ln1_b[i][None, :],
                         router, n_ctx, tm, alpha)
        h1 = outs[0]
        if i % 2 == 0:
            h = _dense_ffn(h1, mod, ffn_w1[j][None].astype(BF16), ffn_w3[j][None].astype(BF16),
                           ffn_w2[j][None].astype(BF16), *g2_ln, n_ctx, tf, alpha)
        else:
            u, route = outs[1], outs[2]
            n_exp = route.shape[1]
            keep = is_lat if last else jnp.ones((b * t,), bool)
            dst, src, tile_expert, n_used = _routing_tables(route, keep, n_exp, te)
            ys = _moe_ffn(u.reshape(b * t, d), src, moe_w1[j].astype(BF16),
                          moe_w3[j].astype(BF16), moe_w2[j].astype(BF16), tile_expert, n_used, te)
            h = _moe_combine(h1, mod, *g2_ln, ys, dst, route, n_ctx, tc, alpha, skip_ctx=last)
    return h if h.shape[1] == seq else h[:, n_ctx:, :]
```

```python
import functools
import math

import jax
import jax.numpy as jnp
from jax import lax
from jax.experimental import pallas as pl
from jax.experimental.pallas import tpu as pltpu

F32 = jnp.float32
BF16 = jnp.bfloat16

HEAD_DIM = 64
ATTN_HEADS = 4
HEAD_W = 2 * HEAD_DIM
ATTN_W = ATTN_HEADS * HEAD_W
GRID_W = 64
RG_BLOCKS = 8
RG_C = 8.0
CONV_W = 4
N_TOP = 2
ROPE_BASE = 10000.0
LN_EPS = 1e-5
RMS_EPS = 1e-5

LANES = 128
SUBLANES = 8
MXU_DIM = 256
VMEM_LIMIT = 56 * 1024 * 1024


def _cparams(sem, vmem=VMEM_LIMIT):
    return pltpu.CompilerParams(dimension_semantics=sem, vmem_limit_bytes=vmem)


def _pick_tile(n, pref):
    t = min(n, pref)
    while t > 8 and (n % t or t % 8):
        t -= 8
    assert n % t == 0
    return t


def _layer_norm(z, g, b):
    mu = jnp.mean(z, -1, keepdims=True)
    zc = z - mu
    var = jnp.mean(zc * zc, -1, keepdims=True)
    return zc * lax.rsqrt(var + LN_EPS) * g + b


def _row_mods(mb_ref, mc_ref, idx, is_ctx):
    return jnp.where(is_ctx, mc_ref[0, idx:idx + 1, :], mb_ref[0, idx:idx + 1, :])


def _sigmoid(x):
    return 0.5 * jnp.tanh(0.5 * x) + 0.5


def _mod_kernel(c_ref, w_ref, b_ref, o_ref):
    cv = c_ref[...]
    s = (cv * _sigmoid(cv)).astype(BF16)
    o_ref[0] = jnp.dot(s, w_ref[0].astype(BF16), preferred_element_type=F32) + b_ref[0]


def _modulation(cvec, w_mod, b_mod):
    depth, d, n = w_mod.shape
    mb = cvec.shape[0]
    tn = _pick_tile(n, 1536)
    return pl.pallas_call(
        _mod_kernel,
        grid=(depth, n // tn),
        in_specs=[
            pl.BlockSpec((mb, d), lambda i, j: (0, 0)),
            pl.BlockSpec((1, d, tn), lambda i, j: (i, 0, j)),
            pl.BlockSpec((1, 1, tn), lambda i, j: (i, 0, j)),
        ],
        out_specs=pl.BlockSpec((1, mb, tn), lambda i, j: (i, 0, j)),
        out_shape=jax.ShapeDtypeStruct((depth, mb, n), F32),
        compiler_params=_cparams(("arbitrary", "arbitrary")),
        name="adaln_mod",
    )(cvec, w_mod, b_mod.reshape(depth, 1, n))


def _rope_store(t, cs, sn, out_ref):
    lane = lax.broadcasted_iota(jnp.int32, (1, LANES), 1)
    first = (lane % HEAD_DIM) < (HEAD_DIM // 2)
    for g in range(ATTN_W // LANES):
        tg = t[:, g * LANES:(g + 1) * LANES]
        sw = jnp.where(first, pltpu.roll(tg, LANES - HEAD_DIM // 2, 1),
                       pltpu.roll(tg, HEAD_DIM // 2, 1))
        out_ref[0, :, g * LANES:(g + 1) * LANES] = (tg * cs + sw * sn).astype(out_ref.dtype)


def _in_kernel(h_ref, mb_ref, mc_ref, w_ref, cs_ref, sn_ref,
               k_ref, v_ref, xr_ref, q_ref, y_ref, *, tm, n_ctx):
    j = pl.program_id(1)
    row = j * tm + lax.broadcasted_iota(jnp.int32, (tm, 1), 0)
    is_ctx = row < n_ctx
    shift = _row_mods(mb_ref, mc_ref, 0, is_ctx)
    scale = _row_mods(mb_ref, mc_ref, 1, is_ctx)
    x = (h_ref[0] * (1.0 + scale) + shift).astype(BF16)
    cs = cs_ref[...]
    sn = sn_ref[...]
    w = ATTN_W

    def proj(part):
        return jnp.dot(x, w_ref[:, part * w:(part + 1) * w], preferred_element_type=F32)

    _rope_store(proj(0), cs, sn, k_ref)
    v_ref[0] = proj(1).astype(v_ref.dtype)
    xr_ref[0] = proj(2)
    _rope_store(proj(3), cs, sn, q_ref)
    y_ref[0] = proj(4)


def _in_proj(h, mod, w_in, cs_tab, sn_tab, n_ctx, tm):
    b, t, d = h.shape
    n = w_in.shape[1]
    w = ATTN_W
    assert n == 5 * w
    row_spec = lambda width: pl.BlockSpec((1, tm, width), lambda i, j: (i, j, 0))
    return pl.pallas_call(
        functools.partial(_in_kernel, tm=tm, n_ctx=n_ctx),
        grid=(b, t // tm),
        in_specs=[
            row_spec(d),
            pl.BlockSpec((1, 6, d), lambda i, j: (i, 0, 0)),
            pl.BlockSpec((1, 6, d), lambda i, j: (b, 0, 0)),
            pl.BlockSpec((d, n), lambda i, j: (0, 0)),
            pl.BlockSpec((tm, LANES), lambda i, j: (j, 0)),
            pl.BlockSpec((tm, LANES), lambda i, j: (j, 0)),
        ],
        out_specs=[row_spec(w)] * 5,
        out_shape=[
            jax.ShapeDtypeStruct((b, t, w), BF16),
            jax.ShapeDtypeStruct((b, t, w), BF16),
            jax.ShapeDtypeStruct((b, t, w), F32),
            jax.ShapeDtypeStruct((b, t, w), BF16),
            jax.ShapeDtypeStruct((b, t, w), F32),
        ],
        compiler_params=_cparams(("arbitrary", "arbitrary")),
        name="in_proj",
    )(h, mod, mod, w_in, cs_tab, sn_tab)


def _attn_kernel(q_ref, k_ref, v_ref, lamv_ref, g_ref, o_ref, *, n_ctx, n_all, lam_init):
    j = pl.program_id(1)
    lv = lamv_ref[...]
    lam = (jnp.exp(jnp.sum(lv[0:1] * lv[1:2], keepdims=True))
           - jnp.exp(jnp.sum(lv[2:3] * lv[3:4], keepdims=True)) + lam_init)
    gain = g_ref[...] * (1.0 - lam_init)
    lane = lax.broadcasted_iota(jnp.int32, (1, HEAD_W), 1)
    dn = (((1,), (1,)), ((), ()))

    def softmax_av(qm, kh, vext):
        s = lax.dot_general(qm, kh, dn, preferred_element_type=F32)
        p = jnp.exp2(s - jnp.max(s, -1, keepdims=True)).astype(BF16)
        oe = jnp.dot(p, vext, preferred_element_type=F32)
        return oe[:, :HEAD_W] / oe[:, HEAD_W:HEAD_W + 1]

    def run(nk):
        ones = jnp.ones((nk, HEAD_W), BF16)
        for h in range(ATTN_HEADS):
            sl = slice(h * HEAD_W, (h + 1) * HEAD_W)
            qh = q_ref[0, :, sl]
            kh = k_ref[0, 0:nk, sl]
            vext = jnp.concatenate([v_ref[0, 0:nk, sl], ones], axis=1)
            q1 = jnp.where(lane < HEAD_DIM, qh, jnp.zeros_like(qh))
            q2 = jnp.where(lane >= HEAD_DIM, qh, jnp.zeros_like(qh))
            o = softmax_av(q1, kh, vext) - lam * softmax_av(q2, kh, vext)
            o = o * lax.rsqrt(jnp.mean(o * o, -1, keepdims=True) + RMS_EPS) * gain
            o_ref[0, :, sl] = o.astype(o_ref.dtype)

    @pl.when(j == 0)
    def _():
        run(n_ctx)

    @pl.when(j > 0)
    def _():
        run(n_all)


def _attention(q, k, v, lamv, gain, n_ctx, lam_init):
    b, t, w = q.shape
    tq = n_ctx
    return pl.pallas_call(
        functools.partial(_attn_kernel, n_ctx=n_ctx, n_all=t, lam_init=lam_init),
        grid=(b, t // tq),
        in_specs=[
            pl.BlockSpec((1, tq, w), lambda i, j: (i, j, 0)),
            pl.BlockSpec((1, t, w), lambda i, j: (i, 0, 0)),
            pl.BlockSpec((1, t, w), lambda i, j: (i, 0, 0)),
            pl.BlockSpec((4, HEAD_DIM), lambda i, j: (0, 0)),
            pl.BlockSpec((1, HEAD_W), lambda i, j: (0, 0)),
        ],
        out_specs=pl.BlockSpec((1, tq, w), lambda i, j: (i, j, 0)),
        out_shape=jax.ShapeDtypeStruct((b, t, w), BF16),
        compiler_params=_cparams(("arbitrary", "arbitrary")),
        name="diff_attn",
    )(q, k, v, lamv, gain)


def _gelu_tanh(x):
    return 0.5 * x * (1.0 + jnp.tanh(math.sqrt(2.0 / math.pi) * (x + 0.044715 * x * x * x)))


def _softplus(z):
    return jnp.maximum(z, 0.0) + jnp.log1p(jnp.exp(-jnp.abs(z)))


def _rg_kernel(xr_ref, y_ref, cw_ref, cb_ref, wg_ref, bias_ref, lam_ref, o_ref,
               xc_sc, acc_sc, a_sl, b_sl, *, n_ctx, n_all, chunk):
    t, c, r = n_all, n_ctx, chunk
    half = MXU_DIM
    rgw = xc_sc.shape[1]
    groups = rgw // LANES
    nseg = SUBLANES
    seg = r // nseg

    row = lax.broadcasted_iota(jnp.int32, (t, 1), 0)
    in_ctx = row < c
    tl = jnp.where(in_ctx, row, row - c)
    span = jnp.where(in_ctx, c, t - c)
    for g in range(groups):
        sl = slice(g * LANES, (g + 1) * LANES)
        xg = xr_ref[0, :, sl]
        wv = cw_ref[:, sl]
        acc = cb_ref[:, sl] + xg * wv[2:3]
        acc = acc + jnp.where(tl >= 2, pltpu.roll(xg, 2, 0), 0.0) * wv[0:1]
        acc = acc + jnp.where(tl >= 1, pltpu.roll(xg, 1, 0), 0.0) * wv[1:2]
        acc = acc + jnp.where(tl + 1 < span, pltpu.roll(xg, t - 1, 0), 0.0) * wv[3:4]
        xc_sc[:, sl] = acc

    n_chunks = t // r
    ctx_chunks = c // r
    for d in range(2):
        c4 = (-0.5 * RG_C) * _softplus(-lam_ref[d:d + 1, :])
        ba = 0.5 * bias_ref[2 * d:2 * d + 1, :]
        bx = 0.5 * bias_ref[2 * d + 1:2 * d + 2, :]

        def chunk_step(s, hc, d=d, c4=c4, ba=ba, bx=bx):
            if d == 0:
                ci = s
            else:
                ci = jnp.where(s < ctx_chunks, ctx_chunks - 1 - s,
                               n_chunks - 1 - (s - ctx_chunks))
            r0 = pl.multiple_of(ci * r, r)
            xc = xc_sc[pl.ds(r0, r), :]
            xb = xc.astype(BF16)
            for hh in range(rgw // half):
                hs = slice(hh * half, (hh + 1) * half)
                g2 = jnp.dot(xb[:, hs], wg_ref[d, hh], preferred_element_type=F32)
                tr = jnp.tanh(g2[:, :half] + ba[:, hs])
                gi = 0.5 * jnp.tanh(g2[:, half:] + bx[:, hs]) + 0.5
                log_a = c4[:, hs] * (tr + 1.0)
                a = jnp.exp(log_a)
                th = jnp.tanh(log_a)
                bt = jnp.sqrt(-2.0 * th) * lax.rsqrt(1.0 - th) * (gi * xc[:, hs])
                for gg in range(half // LANES):
                    g = hh * (half // LANES) + gg
                    ls = slice(gg * LANES, (gg + 1) * LANES)
                    for k in range(nseg):
                        rows = slice(k * seg, (k + 1) * seg)
                        a_sl[g, pl.ds(k, seg, stride=nseg), :] = a[rows, ls]
                        b_sl[g, pl.ds(k, seg, stride=nseg), :] = bt[rows, ls]

            def seg_step(ii, carry):
                i = ii if d == 0 else seg - 1 - ii
                base = pl.multiple_of(i * nseg, nseg)
                hs_, as_ = carry
                nh, na = [], []
                for g in range(groups):
                    av = a_sl[g, pl.ds(base, nseg), :]
                    hv = av * hs_[g] + b_sl[g, pl.ds(base, nseg), :]
                    pv = av * as_[g]
                    b_sl[g, pl.ds(base, nseg), :] = hv
                    a_sl[g, pl.ds(base, nseg), :] = pv
                    nh.append(hv)
                    na.append(pv)
                return tuple(nh), tuple(na)

            zero = jnp.zeros((nseg, LANES), F32)
            one = jnp.ones((nseg, LANES), F32)
            h_end, a_end = lax.fori_loop(
                0, seg, seg_step, ((zero,) * groups, (one,) * groups), unroll=4)

            order = range(nseg) if d == 0 else range(nseg - 1, -1, -1)
            new_carry = []
            for g in range(groups):
                ls = slice(g * LANES, (g + 1) * LANES)
                h_in = hc[:, ls]
                for k in order:
                    rows = pl.ds(r0 + k * seg, seg)
                    h_true = (b_sl[g, pl.ds(k, seg, stride=nseg), :]
                              + a_sl[g, pl.ds(k, seg, stride=nseg), :] * h_in)
                    if d == 0:
                        acc_sc[rows, ls] = h_true
                    else:
                        tot = acc_sc[rows, ls] + h_true
                        o_ref[0, rows, ls] = (tot * _gelu_tanh(y_ref[0, rows, ls])).astype(o_ref.dtype)
                    h_in = h_end[g][k:k + 1, :] + a_end[g][k:k + 1, :] * h_in
                new_carry.append(h_in)
            return jnp.concatenate(new_carry, axis=1)

        lax.fori_loop(0, n_chunks, chunk_step, jnp.zeros((1, rgw), F32))


def _rglru(xr, y, conv_w, conv_b, wg, bias, lam, n_ctx, chunk):
    b, t, w = xr.shape
    full = lambda a: pl.BlockSpec(a.shape, lambda i: (0,) * a.ndim)
    seq = pl.BlockSpec((1, t, w), lambda i: (i, 0, 0))
    return pl.pallas_call(
        functools.partial(_rg_kernel, n_ctx=n_ctx, n_all=t, chunk=chunk),
        grid=(b,),
        in_specs=[seq, seq, full(conv_w), full(conv_b), full(wg), full(bias), full(lam)],
        out_specs=seq,
        out_shape=jax.ShapeDtypeStruct((b, t, w), BF16),
        scratch_shapes=[
            pltpu.VMEM((t, w), F32),
            pltpu.VMEM((t, w), F32),
            pltpu.VMEM((w // LANES, chunk, LANES), F32),
            pltpu.VMEM((w // LANES, chunk, LANES), F32),
        ],
        compiler_params=_cparams(("arbitrary",)),
        name="rglru",
    )(xr, y, conv_w, conv_b, wg, bias, lam)


def _out_kernel(*refs, tm, n_ctx, alpha, moe):
    if moe:
        (att_ref, rg_ref, h_ref, mb_ref, mc_ref, w_ref, g_ref, b_ref, rt_ref,
         h1_ref, u_ref, route_ref) = refs
    else:
        att_ref, rg_ref, h_ref, mb_ref, mc_ref, w_ref, g_ref, b_ref, h1_ref = refs
    j = pl.program_id(1)
    row = j * tm + lax.broadcasted_iota(jnp.int32, (tm, 1), 0)
    is_ctx = row < n_ctx
    aw = att_ref.shape[2]
    mix = (jnp.dot(att_ref[0], w_ref[0:aw, :], preferred_element_type=F32)
           + jnp.dot(rg_ref[0], w_ref[aw:, :], preferred_element_type=F32))
    g1 = _row_mods(mb_ref, mc_ref, 2, is_ctx)
    h1 = _layer_norm(alpha * h_ref[0] + (1.0 + g1) * mix, g_ref[...], b_ref[...])
    h1_ref[0] = h1
    if not moe:
        return
    sh2 = _row_mods(mb_ref, mc_ref, 3, is_ctx)
    sc2 = _row_mods(mb_ref, mc_ref, 4, is_ctx)
    u = h1 * (1.0 + sc2) + sh2
    u_ref[0] = u
    n_exp = route_ref.shape[1]
    u_hi = u.astype(BF16)
    u_lo = (u - u_hi.astype(F32)).astype(BF16)
    prod = (jnp.dot(u_hi, rt_ref[...], preferred_element_type=F32)
            + jnp.dot(u_lo, rt_ref[...], preferred_element_type=F32))
    logits = prod[:, :LANES] + prod[:, LANES:]
    lane = lax.broadcasted_iota(jnp.int32, (1, LANES), 1)
    neg = -jnp.inf
    lg = jnp.where(lane < n_exp, logits, neg)
    m1 = jnp.max(lg, -1, keepdims=True)
    i1 = jnp.min(jnp.where(lg == m1, lane, LANES), -1, keepdims=True)
    lg2 = jnp.where(lane == i1, neg, lg)
    m2 = jnp.max(lg2, -1, keepdims=True)
    i2 = jnp.min(jnp.where(lg2 == m2, lane, LANES), -1, keepdims=True)
    e2 = jnp.exp(m2 - m1)
    den = 1.0 + e2
    route = jnp.where(lane == 0, i1.astype(F32),
                      jnp.where(lane == 1, i2.astype(F32),
                                jnp.where(lane == 2, 1.0 / den,
                                          jnp.where(lane == 3, e2 / den, 0.0))))
    route_ref[...] = route[:, :n_exp]


def _out_proj(att, rg, h, mod, w_out, ln_g, ln_b, router, n_ctx, tm, alpha):
    b, t, d = h.shape
    aw = att.shape[2]
    moe = router is not None
    row_spec = lambda width: pl.BlockSpec((1, tm, width), lambda i, j: (i, j, 0))
    vec = pl.BlockSpec((1, d), lambda i, j: (0, 0))
    in_specs = [
        row_spec(aw), row_spec(aw), row_spec(d),
        pl.BlockSpec((1, 6, d), lambda i, j: (i, 0, 0)),
        pl.BlockSpec((1, 6, d), lambda i, j: (b, 0, 0)),
        pl.BlockSpec((d, d), lambda i, j: (0, 0)),
        vec, vec,
    ]
    args = [att, rg, h, mod, mod, w_out, ln_g, ln_b]
    out_specs = [row_spec(d)]
    out_shape = [jax.ShapeDtypeStruct((b, t, d), F32)]
    if moe:
        n_exp = router.shape[1]
        r_hi = router.astype(BF16)
        r_lo = (router - r_hi.astype(F32)).astype(BF16)
        pad = lambda a: jnp.zeros((d, LANES), BF16).at[:, :n_exp].set(a)
        in_specs.append(pl.BlockSpec((d, 2 * LANES), lambda i, j: (0, 0)))
        args.append(jnp.concatenate([pad(r_hi), pad(r_lo)], axis=1))
        tiles = t // tm
        out_specs += [row_spec(d), pl.BlockSpec((tm, n_exp), lambda i, j: (i * tiles + j, 0))]
        out_shape += [jax.ShapeDtypeStruct((b, t, d), F32),
                      jax.ShapeDtypeStruct((b * t, n_exp), F32)]
    return pl.pallas_call(
        functools.partial(_out_kernel, tm=tm, n_ctx=n_ctx, alpha=alpha, moe=moe),
        grid=(b, t // tm),
        in_specs=in_specs,
        out_specs=out_specs,
        out_shape=out_shape,
        compiler_params=_cparams(("arbitrary", "arbitrary")),
        name="out_proj_moe" if moe else "out_proj",
    )(*args)


def _swiglu(xs, w1_ref, w3_ref, w2_ref, fc, between=None):
    f = w1_ref.shape[2]
    n_chunks = f // fc
    acc = None
    for c in range(n_chunks):
        cs = slice(c * fc, (c + 1) * fc)
        x = xs[...]
        a = jnp.dot(x, w1_ref[0, :, cs], preferred_element_type=F32)
        bgate = jnp.dot(x, w3_ref[0, :, cs], preferred_element_type=F32)
        if between is not None:
            between(c, n_chunks)
        gact = (a * _sigmoid(a) * bgate).astype(BF16)
        part = jnp.dot(gact, w2_ref[0, cs, :], preferred_element_type=F32)
        acc = part if acc is None else acc + part
    return acc


def _ffn_chunk(f):
    return MXU_DIM if f % MXU_DIM == 0 else f


def _dense_ffn_kernel(h1_ref, mb_ref, mc_ref, w1_ref, w3_ref, w2_ref, g_ref, b_ref, o_ref,
                      *, tf, n_ctx, alpha, fc):
    row = pl.program_id(1) * tf + lax.broadcasted_iota(jnp.int32, (tf, 1), 0)
    is_ctx = row < n_ctx
    sh2 = _row_mods(mb_ref, mc_ref, 3, is_ctx)
    sc2 = _row_mods(mb_ref, mc_ref, 4, is_ctx)
    g2 = _row_mods(mb_ref, mc_ref, 5, is_ctx)
    h1 = h1_ref[0]
    u = (h1 * (1.0 + sc2) + sh2).astype(BF16)
    mixed = _swiglu(u, w1_ref, w3_ref, w2_ref, fc)
    o_ref[0] = _layer_norm(alpha * h1 + (1.0 + g2) * mixed, g_ref[...], b_ref[...])


def _dense_ffn(h1, mod, w1, w3, w2, ln_g, ln_b, n_ctx, tf, alpha):
    b, t, d = h1.shape
    f = w1.shape[2]
    row_spec = pl.BlockSpec((1, tf, d), lambda i, j: (i, j, 0))
    vec = pl.BlockSpec((1, d), lambda i, j: (0, 0))
    const3 = lambda a: pl.BlockSpec(a.shape, lambda i, j: (0, 0, 0))
    return pl.pallas_call(
        functools.partial(_dense_ffn_kernel, tf=tf, n_ctx=n_ctx, alpha=alpha, fc=_ffn_chunk(f)),
        grid=(b, t // tf),
        in_specs=[
            row_spec,
            pl.BlockSpec((1, 6, d), lambda i, j: (i, 0, 0)),
            pl.BlockSpec((1, 6, d), lambda i, j: (b, 0, 0)),
            const3(w1), const3(w3), const3(w2), vec, vec,
        ],
        out_specs=row_spec,
        out_shape=jax.ShapeDtypeStruct((b, t, d), F32),
        compiler_params=_cparams(("arbitrary", "arbitrary")),
        name="dense_ffn",
    )(h1, mod, mod, w1, w3, w2, ln_g, ln_b)


def _moe_ffn_kernel(te_ref, nu_ref, src_ref, nxt_ref, u_ref, w1_ref, w3_ref, w2_ref, o_ref,
                    xbuf, xb_sc, sem, *, tm, fc):
    del te_ref
    i = pl.program_id(0)
    last = pl.num_programs(0) - 1
    n_used = nu_ref[0]
    slot = i % 2

    def row_copy(idx_ref, dst_slot, r):
        return pltpu.make_async_copy(u_ref.at[pl.ds(idx_ref[0, 0, r], 1)],
                                     xbuf.at[dst_slot, pl.ds(r, 1)], sem.at[dst_slot])

    def gather_loop(idx_ref, dst_slot):
        def issue(r, carry):
            row_copy(idx_ref, dst_slot, r).start()
            return carry
        lax.fori_loop(0, tm, issue, 0, unroll=8)

    def wait_tile(dst_slot):
        pltpu.make_async_copy(u_ref.at[pl.ds(0, tm)], xbuf.at[dst_slot], sem.at[dst_slot]).wait()

    @pl.when(i == 0)
    def _():
        gather_loop(src_ref, 0)

    wait_tile(slot)

    @pl.when(i < n_used)
    def _():
        def issue_share(c, n_chunks):
            per = -(-tm // n_chunks)
            for r in range(c * per, min((c + 1) * per, tm)):
                row_copy(nxt_ref, 1 - slot, r).start()
        xb_sc[...] = xbuf[slot].astype(BF16)
        o_ref[...] = _swiglu(xb_sc, w1_ref, w3_ref, w2_ref, fc, issue_share)

    @pl.when(i >= n_used)
    def _():
        gather_loop(nxt_ref, 1 - slot)
        o_ref[...] = jnp.zeros_like(o_ref)

    @pl.when(i == last)
    def _():
        wait_tile(1 - slot)


def _moe_ffn(u, src, w1, w3, w2, tile_expert, n_used, tm):
    n, d = u.shape
    f = w1.shape[2]
    n_tiles = src.shape[0] // tm
    src3 = src.reshape(n_tiles, 1, tm)
    wspec = lambda shape: pl.BlockSpec(shape, lambda i, te, nu: (te[i], 0, 0))
    grid_spec = pltpu.PrefetchScalarGridSpec(
        num_scalar_prefetch=2,
        grid=(n_tiles,),
        in_specs=[
            pl.BlockSpec((1, 1, tm), lambda i, te, nu: (i, 0, 0), memory_space=pltpu.SMEM),
            pl.BlockSpec((1, 1, tm), lambda i, te, nu: (jnp.minimum(i + 1, n_tiles - 1), 0, 0),
                         memory_space=pltpu.SMEM),
            pl.BlockSpec(memory_space=pl.ANY),
            wspec((1, d, f)), wspec((1, d, f)), wspec((1, f, d)),
        ],
        out_specs=pl.BlockSpec((tm, d), lambda i, te, nu: (i, 0)),
        scratch_shapes=[pltpu.VMEM((2, tm, d), F32), pltpu.VMEM((tm, d), BF16),
                        pltpu.SemaphoreType.DMA((2,))],
    )
    return pl.pallas_call(
        functools.partial(_moe_ffn_kernel, tm=tm, fc=_ffn_chunk(f)),
        grid_spec=grid_spec,
        out_shape=jax.ShapeDtypeStruct((n_tiles * tm, d), F32),
        compiler_params=_cparams(("arbitrary",)),
        name="moe_ffn",
    )(tile_expert, n_used, src3, src3, u, w1, w3, w2)


def _combine_kernel(dst_ref, nxt_ref, ys_ref, route_ref, h1_ref, mb_ref, mc_ref, g_ref, b_ref,
                    o_ref, buf, sem, *, tc, n_ctx, first_tile, alpha):
    i, j = pl.program_id(0), pl.program_id(1)
    nj = pl.num_programs(1)
    s = i * nj + j
    total = pl.num_programs(0) * nj
    slot = s % 2

    def gather(idx_ref, dslot):
        def issue(r, carry):
            for k in range(N_TOP):
                pltpu.make_async_copy(ys_ref.at[pl.ds(idx_ref[0, 0, N_TOP * r + k], 1)],
                                      buf.at[dslot, pl.ds(k * tc + r, 1)],
                                      sem.at[dslot]).start(priority=k)
            return carry
        lax.fori_loop(0, tc, issue, 0, unroll=4)

    @pl.when(s == 0)
    def _():
        gather(dst_ref, 0)

    @pl.when(s + 1 < total)
    def _():
        gather(nxt_ref, 1 - slot)

    pltpu.make_async_copy(ys_ref.at[pl.ds(0, N_TOP * tc)], buf.at[slot], sem.at[slot]).wait()
    rt = route_ref[...]
    mixed = rt[:, 2:3] * buf[slot, 0:tc] + rt[:, 3:4] * buf[slot, tc:N_TOP * tc]
    row = (first_tile + j) * tc + lax.broadcasted_iota(jnp.int32, (tc, 1), 0)
    g2 = _row_mods(mb_ref, mc_ref, 5, row < n_ctx)
    o_ref[0] = _layer_norm(alpha * h1_ref[0] + (1.0 + g2) * mixed, g_ref[...], b_ref[...])


def _moe_combine(h1, mod, ln_g, ln_b, ys, dst, route, n_ctx, tc, alpha, skip_ctx):
    b, t, d = h1.shape
    tiles = t // tc
    off = n_ctx // tc if skip_ctx else 0
    nj = tiles - off
    n_exp = route.shape[1]
    dst3 = dst.reshape(b * tiles, 1, N_TOP * tc)

    def cur(i, j):
        return i * tiles + off + j

    def nxt(i, j):
        i2 = jnp.minimum(i + (j + 1) // nj, b - 1)
        return i2 * tiles + off + (j + 1) % nj

    vec = pl.BlockSpec((1, d), lambda i, j: (0, 0))
    return pl.pallas_call(
        functools.partial(_combine_kernel, tc=tc, n_ctx=n_ctx, first_tile=off, alpha=alpha),
        grid=(b, nj),
        in_specs=[
            pl.BlockSpec((1, 1, N_TOP * tc), lambda i, j: (cur(i, j), 0, 0),
                         memory_space=pltpu.SMEM),
            pl.BlockSpec((1, 1, N_TOP * tc), lambda i, j: (nxt(i, j), 0, 0),
                         memory_space=pltpu.SMEM),
            pl.BlockSpec(memory_space=pl.ANY),
            pl.BlockSpec((tc, n_exp), lambda i, j: (cur(i, j), 0)),
            pl.BlockSpec((1, tc, d), lambda i, j: (i, off + j, 0)),
            pl.BlockSpec((1, 6, d), lambda i, j: (i, 0, 0)),
            pl.BlockSpec((1, 6, d), lambda i, j: (b, 0, 0)),
            vec, vec,
        ],
        out_specs=pl.BlockSpec((1, tc, d), lambda i, j: (i, j, 0)),
        out_shape=jax.ShapeDtypeStruct((b, nj * tc, d), F32),
        scratch_shapes=[pltpu.VMEM((2, N_TOP * tc, d), F32), pltpu.SemaphoreType.DMA((2,))],
        compiler_params=_cparams(("arbitrary", "arbitrary")),
        name="moe_combine",
    )(dst3, dst3, ys, route, h1, mod, mod, ln_g, ln_b)


def _rope_tables(n_ctx, seq):
    rows = seq // GRID_W
    pairs = HEAD_DIM // 4
    rpos = jnp.repeat(jnp.arange(rows, dtype=F32), GRID_W)
    cpos = jnp.tile(jnp.arange(GRID_W, dtype=F32), rows)
    inv_freq = ROPE_BASE ** (-jnp.arange(pairs, dtype=F32) / pairs)
    ang = jnp.concatenate([rpos[:, None] * inv_freq, cpos[:, None] * inv_freq], -1)
    cos, sin = jnp.cos(ang), jnp.sin(ang)
    reps = LANES // HEAD_DIM
    cs = jnp.tile(jnp.concatenate([cos, cos], -1), (1, reps))
    sn = jnp.tile(jnp.concatenate([-sin, sin], -1), (1, reps))
    cs = jnp.concatenate([jnp.ones((n_ctx, LANES), F32), cs], 0)
    sn = jnp.concatenate([jnp.zeros((n_ctx, LANES), F32), sn], 0)
    return cs, sn


def _prep_w_in(w_in):
    n = w_in.shape[-1]
    half = HEAD_DIM // 2
    within = jnp.concatenate([2 * jnp.arange(half), 2 * jnp.arange(half) + 1])
    perm = jnp.arange(n)
    scale = jnp.ones((n,), F32)
    q0 = 3 * ATTN_W
    for base in (0, q0):
        blk = (base + HEAD_DIM * jnp.arange(ATTN_W // HEAD_DIM)[:, None] + within[None, :]).reshape(-1)
        perm = perm.at[base:base + ATTN_W].set(blk)
    scale = scale.at[q0:q0 + ATTN_W].set(HEAD_DIM ** -0.5 * math.log2(math.e))
    return (w_in[..., perm] * scale).astype(BF16)


def _prep_rg_gates(wa, wx):
    nd, g, bw, _ = wa.shape
    per = MXU_DIM // bw
    halves = g // per

    def bdiag(w):
        w = w.reshape(nd, halves, per, bw, bw)
        eye = jnp.eye(per, dtype=w.dtype)
        full = w[:, :, :, :, None, :] * eye[None, None, :, None, :, None]
        return full.reshape(nd, halves, per * bw, per * bw)

    return (0.5 * jnp.concatenate([bdiag(wa), bdiag(wx)], -1)).astype(BF16)


def _routing_tables(route, keep, n_exp, tm):
    e = route[:, :N_TOP].astype(jnp.int32).reshape(-1)
    keep2 = jnp.repeat(keep, N_TOP)
    oh = ((e[:, None] == jnp.arange(n_exp)[None, :]) & keep2[:, None]).astype(jnp.int32)
    csum = jnp.cumsum(oh, axis=0)
    rank = jnp.sum((csum - oh) * oh, axis=1)
    counts = csum[-1]
    tiles_per = (counts + tm - 1) // tm
    tile_end = jnp.cumsum(tiles_per)
    off = (tile_end - tiles_per) * tm
    n_tiles = e.shape[0] // tm + n_exp
    rows = n_tiles * tm
    dst = jnp.where(keep2, jnp.sum(oh * off[None, :], axis=1) + rank, rows).astype(jnp.int32)
    tile_ids = jnp.arange(n_tiles)
    tile_expert = jnp.sum((tile_ids[:, None] >= tile_end[None, :]).astype(jnp.int32), axis=1)
    tile_expert = jnp.minimum(tile_expert, n_exp - 1).astype(jnp.int32)
    order = jnp.argsort(jnp.where(keep2, e, n_exp), stable=True).astype(jnp.int32)
    row_expert = jnp.repeat(tile_expert, tm)
    k_in = jnp.arange(rows, dtype=jnp.int32) - off[row_expert]
    first = jnp.cumsum(counts) - counts
    slot = jnp.clip(first[row_expert] + k_in, 0, e.shape[0] - 1)
    src = jnp.where(k_in < counts[row_expert], order[slot] // N_TOP, 0).astype(jnp.int32)
    return dst, src, tile_expert, tile_end[-1:].astype(jnp.int32)


def kernel(x, c, ctx, c_ctx, w_mod, b_mod, w_in, lam_q1, lam_k1, lam_q2, lam_k2, subln_g,
           conv_w, conv_b, rg_wa, rg_ba, rg_wx, rg_bx, rg_lambda, w_out,
           ln1_g, ln1_b, ln2_g, ln2_b, ffn_w1, ffn_w3, ffn_w2,
           moe_router, moe_w1, moe_w3, moe_w2):
    b, seq, d = x.shape
    n_ctx = ctx.shape[1]
    t = n_ctx + seq
    depth = w_in.shape[0]
    alpha = (2 * depth) ** 0.25
    assert seq % GRID_W == 0 and d - ATTN_W == ATTN_W and t % n_ctx == 0

    tm = _pick_tile(t, 768)
    tf = _pick_tile(t, 768)
    te = 512 if (N_TOP * b * t) % 512 == 0 else _pick_tile(N_TOP * b * t, 512)
    tc = _pick_tile(n_ctx, 256)
    chunk = math.gcd(n_ctx, 256)

    h = jnp.concatenate([ctx, x], axis=1)
    mrows = -(-(b + 1) // SUBLANES) * SUBLANES
    cvec = jnp.zeros((mrows, d), F32).at[:b].set(c).at[b].set(c_ctx)
    mod_all = _modulation(cvec, w_mod, b_mod).reshape(depth, mrows, 6, d)
    cs_tab, sn_tab = _rope_tables(n_ctx, seq)

    w_in_p = _prep_w_in(w_in)
    w_out_b = w_out.astype(BF16)
    lamv = jnp.stack([lam_q1, lam_k1, lam_q2, lam_k2], axis=1)
    rg_bias = jnp.stack([rg_ba, rg_bx], axis=2).reshape(depth, 4, -1)
    is_lat = jnp.tile(jnp.arange(t) >= n_ctx, b)

    for i in range(depth):
        last = i == depth - 1
        lam_init = 0.8 - 0.6 * math.exp(-0.3 * i)
        mod = mod_all[i]
        g2_ln = (ln2_g[i][None, :], ln2_b[i][None, :])
        k, v, xr, q, y = _in_proj(h, mod, w_in_p[i], cs_tab, sn_tab, n_ctx, tm)
        att = _attention(q, k, v, lamv[i], subln_g[i][None, :], n_ctx, lam_init)
        rg = _rglru(xr, y, conv_w[i], conv_b[i][None, :], _prep_rg_gates(rg_wa[i], rg_wx[i]),
                    rg_bias[i], rg_lambda[i], n_ctx, chunk)
        j = i // 2
        router = moe_router[j] if i % 2 else None
        outs = _out_proj(att, rg, h, mod, w_out_b[i], ln1_g[i][None, :], ln1_b[i][None, :],
                         router, n_ctx, tm, alpha)
        h1 = outs[0]
        if i % 2 == 0:
            h = _dense_ffn(h1, mod, ffn_w1[j][None].astype(BF16), ffn_w3[j][None].astype(BF16),
                           ffn_w2[j][None].astype(BF16), *g2_ln, n_ctx, tf, alpha)
        else:
            u, route = outs[1], outs[2]
            n_exp = route.shape[1]
            keep = is_lat if last else jnp.ones((b * t,), bool)
            dst, src, tile_expert, n_used = _routing_tables(route, keep, n_exp, te)
            ys = _moe_ffn(u.reshape(b * t, d), src, moe_w1[j].astype(BF16),
                          moe_w3[j].astype(BF16), moe_w2[j].astype(BF16), tile_expert, n_used, te)
            h = _moe_combine(h1, mod, *g2_ln, ys, dst, route, n_ctx, tc, alpha, skip_ctx=last)
    return h if h.shape[1] == seq else h[:, n_ctx:, :]
```

```python
import functools
import math

import jax
import jax.numpy as jnp
from jax import lax
from jax.experimental import pallas as pl
from jax.experimental.pallas import tpu as pltpu

F32 = jnp.float32
BF16 = jnp.bfloat16

HEAD_DIM = 64
ATTN_HEADS = 4
HEAD_W = 2 * HEAD_DIM
ATTN_W = ATTN_HEADS * HEAD_W
GRID_W = 64
RG_BLOCKS = 8
RG_C = 8.0
CONV_W = 4
N_TOP = 2
ROPE_BASE = 10000.0
LN_EPS = 1e-5
RMS_EPS = 1e-5

LANES = 128
SUBLANES = 8
MXU_DIM = 256
VMEM_LIMIT = 56 * 1024 * 1024


def _cparams(sem, vmem=VMEM_LIMIT):
    return pltpu.CompilerParams(dimension_semantics=sem, vmem_limit_bytes=vmem)


def _pick_tile(n, pref):
    t = min(n, pref)
    while t > 8 and (n % t or t % 8):
        t -= 8
    assert n % t == 0
    return t


def _layer_norm(z, g, b):
    mu = jnp.mean(z, -1, keepdims=True)
    zc = z - mu
    var = jnp.mean(zc * zc, -1, keepdims=True)
    return zc * lax.rsqrt(var + LN_EPS) * g + b


def _row_mods(mb_ref, mc_ref, idx, is_ctx):
    return jnp.where(is_ctx, mc_ref[0, idx:idx + 1, :], mb_ref[0, idx:idx + 1, :])


def _sigmoid(x):
    return 0.5 * jnp.tanh(0.5 * x) + 0.5


def _mod_kernel(c_ref, w_ref, b_ref, o_ref):
    cv = c_ref[...]
    s = (cv * _sigmoid(cv)).astype(BF16)
    o_ref[0] = jnp.dot(s, w_ref[0].astype(BF16), preferred_element_type=F32) + b_ref[0]


def _modulation(cvec, w_mod, b_mod):
    depth, d, n = w_mod.shape
    mb = cvec.shape[0]
    tn = _pick_tile(n, 1536)
    return pl.pallas_call(
        _mod_kernel,
        grid=(depth, n // tn),
        in_specs=[
            pl.BlockSpec((mb, d), lambda i, j: (0, 0)),
            pl.BlockSpec((1, d, tn), lambda i, j: (i, 0, j)),
            pl.BlockSpec((1, 1, tn), lambda i, j: (i, 0, j)),
        ],
        out_specs=pl.BlockSpec((1, mb, tn), lambda i, j: (i, 0, j)),
        out_shape=jax.ShapeDtypeStruct((depth, mb, n), F32),
        compiler_params=_cparams(("arbitrary", "arbitrary")),
        name="adaln_mod",
    )(cvec, w_mod, b_mod.reshape(depth, 1, n))


def _rope_store(t, cs, sn, out_ref):
    lane = lax.broadcasted_iota(jnp.int32, (1, LANES), 1)
    first = (lane % HEAD_DIM) < (HEAD_DIM // 2)
    for g in range(ATTN_W // LANES):
        tg = t[:, g * LANES:(g + 1) * LANES]
        sw = jnp.where(first, pltpu.roll(tg, LANES - HEAD_DIM // 2, 1),
                       pltpu.roll(tg, HEAD_DIM // 2, 1))
        out_ref[0, :, g * LANES:(g + 1) * LANES] = (tg * cs + sw * sn).astype(out_ref.dtype)


def _in_kernel(h_ref, mb_ref, mc_ref, w_ref, cs_ref, sn_ref,
               k_ref, v_ref, xr_ref, q_ref, y_ref, *, tm, n_ctx):
    j = pl.program_id(1)
    row = j * tm + lax.broadcasted_iota(jnp.int32, (tm, 1), 0)
    is_ctx = row < n_ctx
    shift = _row_mods(mb_ref, mc_ref, 0, is_ctx)
    scale = _row_mods(mb_ref, mc_ref, 1, is_ctx)
    x = (h_ref[0] * (1.0 + scale) + shift).astype(BF16)
    cs = cs_ref[...]
    sn = sn_ref[...]
    w = ATTN_W

    def proj(part):
        return jnp.dot(x, w_ref[:, part * w:(part + 1) * w], preferred_element_type=F32)

    _rope_store(proj(0), cs, sn, k_ref)
    v_ref[0] = proj(1).astype(v_ref.dtype)
    xr_ref[0] = proj(2)
    _rope_store(proj(3), cs, sn, q_ref)
    y_ref[0] = proj(4)


def _in_proj(h, mod, w_in, cs_tab, sn_tab, n_ctx, tm):
    b, t, d = h.shape
    n = w_in.shape[1]
    w = ATTN_W
    assert n == 5 * w
    row_spec = lambda width: pl.BlockSpec((1, tm, width), lambda i, j: (i, j, 0))
    return pl.pallas_call(
        functools.partial(_in_kernel, tm=tm, n_ctx=n_ctx),
        grid=(b, t // tm),
        in_specs=[
            row_spec(d),
            pl.BlockSpec((1, 6, d), lambda i, j: (i, 0, 0)),
            pl.BlockSpec((1, 6, d), lambda i, j: (b, 0, 0)),
            pl.BlockSpec((d, n), lambda i, j: (0, 0)),
            pl.BlockSpec((tm, LANES), lambda i, j: (j, 0)),
            pl.BlockSpec((tm, LANES), lambda i, j: (j, 0)),
        ],
        out_specs=[row_spec(w)] * 5,
        out_shape=[
            jax.ShapeDtypeStruct((b, t, w), BF16),
            jax.ShapeDtypeStruct((b, t, w), BF16),
            jax.ShapeDtypeStruct((b, t, w), F32),
            jax.ShapeDtypeStruct((b, t, w), BF16),
            jax.ShapeDtypeStruct((b, t, w), F32),
        ],
        compiler_params=_cparams(("arbitrary", "arbitrary")),
        name="in_proj",
    )(h, mod, mod, w_in, cs_tab, sn_tab)


def _attn_kernel(q_ref, k_ref, v_ref, lamv_ref, g_ref, o_ref, *, n_ctx, n_all, lam_init):
    j = pl.program_id(1)
    lv = lamv_ref[...]
    lam = (jnp.exp(jnp.sum(lv[0:1] * lv[1:2], keepdims=True))
           - jnp.exp(jnp.sum(lv[2:3] * lv[3:4], keepdims=True)) + lam_init)
    gain = g_ref[...] * (1.0 - lam_init)
    lane = lax.broadcasted_iota(jnp.int32, (1, HEAD_W), 1)
    dn = (((1,), (1,)), ((), ()))

    def softmax_av(qm, kh, vext):
        s = lax.dot_general(qm, kh, dn, preferred_element_type=F32)
        p = jnp.exp2(s - jnp.max(s, -1, keepdims=True)).astype(BF16)
        oe = jnp.dot(p, vext, preferred_element_type=F32)
        return oe[:, :HEAD_W] / oe[:, HEAD_W:HEAD_W + 1]

    def run(nk):
        ones = jnp.ones((nk, HEAD_W), BF16)
        for h in range(ATTN_HEADS):
            sl = slice(h * HEAD_W, (h + 1) * HEAD_W)
            qh = q_ref[0, :, sl]
            kh = k_ref[0, 0:nk, sl]
            vext = jnp.concatenate([v_ref[0, 0:nk, sl], ones], axis=1)
            q1 = jnp.where(lane < HEAD_DIM, qh, jnp.zeros_like(qh))
            q2 = jnp.where(lane >= HEAD_DIM, qh, jnp.zeros_like(qh))
            o = softmax_av(q1, kh, vext) - lam * softmax_av(q2, kh, vext)
            o = o * lax.rsqrt(jnp.mean(o * o, -1, keepdims=True) + RMS_EPS) * gain
            o_ref[0, :, sl] = o.astype(o_ref.dtype)

    def run_small(nk):
        ones = jnp.ones((nk, HEAD_W), BF16)
        sls = [slice(h * HEAD_W, (h + 1) * HEAD_W) for h in range(ATTN_HEADS)]
        masks = (lane < HEAD_DIM, lane >= HEAD_DIM)
        ss = []
        for sl in sls:
            qh = q_ref[0, :, sl]
            kh = k_ref[0, 0:nk, sl]
            for mk in masks:
                ss.append(lax.dot_general(jnp.where(mk, qh, jnp.zeros_like(qh)), kh, dn,
                                          preferred_element_type=F32))
        mx = [jnp.max(s, -1, keepdims=True) for s in ss]
        ps = [jnp.exp2(s - m).astype(BF16) for s, m in zip(ss, mx)]
        vexts = [jnp.concatenate([v_ref[0, 0:nk, sl], ones], axis=1) for sl in sls]
        oes = [jnp.dot(p, vexts[n // 2], preferred_element_type=F32) for n, p in enumerate(ps)]
        ons = [oe[:, :HEAD_W] / oe[:, HEAD_W:HEAD_W + 1] for oe in oes]
        os_ = [ons[2 * h] - lam * ons[2 * h + 1] for h in range(ATTN_HEADS)]
        ms = [jnp.mean(o * o, -1, keepdims=True) for o in os_]
        for h, sl in enumerate(sls):
            o_ref[0, :, sl] = (os_[h] * lax.rsqrt(ms[h] + RMS_EPS) * gain).astype(o_ref.dtype)

    @pl.when(j == 0)
    def _():
        run_small(n_ctx)

    @pl.when(j > 0)
    def _():
        run(n_all)


def _attention(q, k, v, lamv, gain, n_ctx, lam_init):
    b, t, w = q.shape
    tq = n_ctx
    return pl.pallas_call(
        functools.partial(_attn_kernel, n_ctx=n_ctx, n_all=t, lam_init=lam_init),
        grid=(b, t // tq),
        in_specs=[
            pl.BlockSpec((1, tq, w), lambda i, j: (i, j, 0)),
            pl.BlockSpec((1, t, w), lambda i, j: (i, 0, 0)),
            pl.BlockSpec((1, t, w), lambda i, j: (i, 0, 0)),
            pl.BlockSpec((4, HEAD_DIM), lambda i, j: (0, 0)),
            pl.BlockSpec((1, HEAD_W), lambda i, j: (0, 0)),
        ],
        out_specs=pl.BlockSpec((1, tq, w), lambda i, j: (i, j, 0)),
        out_shape=jax.ShapeDtypeStruct((b, t, w), BF16),
        compiler_params=_cparams(("arbitrary", "arbitrary")),
        name="diff_attn",
    )(q, k, v, lamv, gain)


def _gelu_tanh(x):
    return 0.5 * x * (1.0 + jnp.tanh(math.sqrt(2.0 / math.pi) * (x + 0.044715 * x * x * x)))


def _softplus(z):
    return jnp.maximum(z, 0.0) + jnp.log1p(jnp.exp(-jnp.abs(z)))


def _rg_kernel(xr_ref, y_ref, cw_ref, cb_ref, wg_ref, bias_ref, lam_ref, o_ref,
               xc_sc, acc_sc, a_sl, b_sl, *, n_ctx, n_all, chunk):
    t, c, r = n_all, n_ctx, chunk
    half = MXU_DIM
    rgw = xc_sc.shape[1]
    groups = rgw // LANES
    nseg = SUBLANES
    seg = r // nseg

    row = lax.broadcasted_iota(jnp.int32, (t, 1), 0)
    in_ctx = row < c
    tl = jnp.where(in_ctx, row, row - c)
    span = jnp.where(in_ctx, c, t - c)
    for g in range(groups):
        sl = slice(g * LANES, (g + 1) * LANES)
        xg = xr_ref[0, :, sl]
        wv = cw_ref[:, sl]
        acc = cb_ref[:, sl] + xg * wv[2:3]
        acc = acc + jnp.where(tl >= 2, pltpu.roll(xg, 2, 0), 0.0) * wv[0:1]
        acc = acc + jnp.where(tl >= 1, pltpu.roll(xg, 1, 0), 0.0) * wv[1:2]
        acc = acc + jnp.where(tl + 1 < span, pltpu.roll(xg, t - 1, 0), 0.0) * wv[3:4]
        xc_sc[:, sl] = acc

    n_chunks = t // r
    ctx_chunks = c // r
    for d in range(2):
        c4 = (-0.5 * RG_C) * _softplus(-lam_ref[d:d + 1, :])
        ba = 0.5 * bias_ref[2 * d:2 * d + 1, :]
        bx = 0.5 * bias_ref[2 * d + 1:2 * d + 2, :]

        def chunk_step(s, hc, d=d, c4=c4, ba=ba, bx=bx):
            if d == 0:
                ci = s
            else:
                ci = jnp.where(s < ctx_chunks, ctx_chunks - 1 - s,
                               n_chunks - 1 - (s - ctx_chunks))
            r0 = pl.multiple_of(ci * r, r)
            xc = xc_sc[pl.ds(r0, r), :]
            xb = xc.astype(BF16)
            for hh in range(rgw // half):
                hs = slice(hh * half, (hh + 1) * half)
                g2 = jnp.dot(xb[:, hs], wg_ref[d, hh], preferred_element_type=F32)
                tr = jnp.tanh(g2[:, :half] + ba[:, hs])
                gi = 0.5 * jnp.tanh(g2[:, half:] + bx[:, hs]) + 0.5
                log_a = c4[:, hs] * (tr + 1.0)
                a = jnp.exp(log_a)
                th = jnp.tanh(log_a)
                bt = jnp.sqrt(-2.0 * th) * lax.rsqrt(1.0 - th) * (gi * xc[:, hs])
                for gg in range(half // LANES):
                    g = hh * (half // LANES) + gg
                    ls = slice(gg * LANES, (gg + 1) * LANES)
                    for k in range(nseg):
                        rows = slice(k * seg, (k + 1) * seg)
                        a_sl[g, pl.ds(k, seg, stride=nseg), :] = a[rows, ls]
                        b_sl[g, pl.ds(k, seg, stride=nseg), :] = bt[rows, ls]

            def seg_step(ii, carry):
                i = ii if d == 0 else seg - 1 - ii
                base = pl.multiple_of(i * nseg, nseg)
                hs_, as_ = carry
                nh, na = [], []
                for g in range(groups):
                    av = a_sl[g, pl.ds(base, nseg), :]
                    hv = av * hs_[g] + b_sl[g, pl.ds(base, nseg), :]
                    pv = av * as_[g]
                    b_sl[g, pl.ds(base, nseg), :] = hv
                    a_sl[g, pl.ds(base, nseg), :] = pv
                    nh.append(hv)
                    na.append(pv)
                return tuple(nh), tuple(na)

            zero = jnp.zeros((nseg, LANES), F32)
            one = jnp.ones((nseg, LANES), F32)
            h_end, a_end = lax.fori_loop(
                0, seg, seg_step, ((zero,) * groups, (one,) * groups), unroll=4)

            order = range(nseg) if d == 0 else range(nseg - 1, -1, -1)
            new_carry = []
            for g in range(groups):
                ls = slice(g * LANES, (g + 1) * LANES)
                h_in = hc[:, ls]
                for k in order:
                    rows = pl.ds(r0 + k * seg, seg)
                    h_true = (b_sl[g, pl.ds(k, seg, stride=nseg), :]
                              + a_sl[g, pl.ds(k, seg, stride=nseg), :] * h_in)
                    if d == 0:
                        acc_sc[rows, ls] = h_true
                    else:
                        tot = acc_sc[rows, ls] + h_true
                        o_ref[0, rows, ls] = (tot * _gelu_tanh(y_ref[0, rows, ls])).astype(o_ref.dtype)
                    h_in = h_end[g][k:k + 1, :] + a_end[g][k:k + 1, :] * h_in
                new_carry.append(h_in)
            return jnp.concatenate(new_carry, axis=1)

        lax.fori_loop(0, n_chunks, chunk_step, jnp.zeros((1, rgw), F32))


def _rglru(xr, y, conv_w, conv_b, wg, bias, lam, n_ctx, chunk):
    b, t, w = xr.shape
    full = lambda a: pl.BlockSpec(a.shape, lambda i: (0,) * a.ndim)
    seq = pl.BlockSpec((1, t, w), lambda i: (i, 0, 0))
    return pl.pallas_call(
        functools.partial(_rg_kernel, n_ctx=n_ctx, n_all=t, chunk=chunk),
        grid=(b,),
        in_specs=[seq, seq, full(conv_w), full(conv_b), full(wg), full(bias), full(lam)],
        out_specs=seq,
        out_shape=jax.ShapeDtypeStruct((b, t, w), BF16),
        scratch_shapes=[
            pltpu.VMEM((t, w), F32),
            pltpu.VMEM((t, w), F32),
            pltpu.VMEM((w // LANES, chunk, LANES), F32),
            pltpu.VMEM((w // LANES, chunk, LANES), F32),
        ],
        compiler_params=_cparams(("arbitrary",)),
        name="rglru",
    )(xr, y, conv_w, conv_b, wg, bias, lam)


def _out_kernel(*refs, tm, n_ctx, alpha, moe):
    if moe:
        (att_ref, rg_ref, h_ref, mb_ref, mc_ref, w_ref, g_ref, b_ref, rt_ref,
         h1_ref, u_ref, route_ref) = refs
    else:
        att_ref, rg_ref, h_ref, mb_ref, mc_ref, w_ref, g_ref, b_ref, h1_ref = refs
    j = pl.program_id(1)
    row = j * tm + lax.broadcasted_iota(jnp.int32, (tm, 1), 0)
    is_ctx = row < n_ctx
    aw = att_ref.shape[2]
    mix = (jnp.dot(att_ref[0], w_ref[0:aw, :], preferred_element_type=F32)
           + jnp.dot(rg_ref[0], w_ref[aw:, :], preferred_element_type=F32))
    g1 = _row_mods(mb_ref, mc_ref, 2, is_ctx)
    h1 = _layer_norm(alpha * h_ref[0] + (1.0 + g1) * mix, g_ref[...], b_ref[...])
    h1_ref[0] = h1
    if not moe:
        return
    sh2 = _row_mods(mb_ref, mc_ref, 3, is_ctx)
    sc2 = _row_mods(mb_ref, mc_ref, 4, is_ctx)
    u = h1 * (1.0 + sc2) + sh2
    u_ref[0] = u
    n_exp = route_ref.shape[1]
    u_hi = u.astype(BF16)
    u_lo = (u - u_hi.astype(F32)).astype(BF16)
    prod = (jnp.dot(u_hi, rt_ref[...], preferred_element_type=F32)
            + jnp.dot(u_lo, rt_ref[...], preferred_element_type=F32))
    logits = prod[:, :LANES] + prod[:, LANES:]
    lane = lax.broadcasted_iota(jnp.int32, (1, LANES), 1)
    neg = -jnp.inf
    lg = jnp.where(lane < n_exp, logits, neg)
    m1 = jnp.max(lg, -1, keepdims=True)
    i1 = jnp.min(jnp.where(lg == m1, lane, LANES), -1, keepdims=True)
    lg2 = jnp.where(lane == i1, neg, lg)
    m2 = jnp.max(lg2, -1, keepdims=True)
    i2 = jnp.min(jnp.where(lg2 == m2, lane, LANES), -1, keepdims=True)
    e2 = jnp.exp(m2 - m1)
    den = 1.0 + e2
    route = jnp.where(lane == 0, i1.astype(F32),
                      jnp.where(lane == 1, i2.astype(F32),
                                jnp.where(lane == 2, 1.0 / den,
                                          jnp.where(lane == 3, e2 / den, 0.0))))
    route_ref[...] = route[:, :n_exp]


def _out_proj(att, rg, h, mod, w_out, ln_g, ln_b, router, n_ctx, tm, alpha):
    b, t, d = h.shape
    aw = att.shape[2]
    moe = router is not None
    row_spec = lambda width: pl.BlockSpec((1, tm, width), lambda i, j: (i, j, 0))
    vec = pl.BlockSpec((1, d), lambda i, j: (0, 0))
    in_specs = [
        row_spec(aw), row_spec(aw), row_spec(d),
        pl.BlockSpec((1, 6, d), lambda i, j: (i, 0, 0)),
        pl.BlockSpec((1, 6, d), lambda i, j: (b, 0, 0)),
        pl.BlockSpec((d, d), lambda i, j: (0, 0)),
        vec, vec,
    ]
    args = [att, rg, h, mod, mod, w_out, ln_g, ln_b]
    out_specs = [row_spec(d)]
    out_shape = [jax.ShapeDtypeStruct((b, t, d), F32)]
    if moe:
        n_exp = router.shape[1]
        r_hi = router.astype(BF16)
        r_lo = (router - r_hi.astype(F32)).astype(BF16)
        pad = lambda a: jnp.zeros((d, LANES), BF16).at[:, :n_exp].set(a)
        in_specs.append(pl.BlockSpec((d, 2 * LANES), lambda i, j: (0, 0)))
        args.append(jnp.concatenate([pad(r_hi), pad(r_lo)], axis=1))
        tiles = t // tm
        out_specs += [row_spec(d), pl.BlockSpec((tm, n_exp), lambda i, j: (i * tiles + j, 0))]
        out_shape += [jax.ShapeDtypeStruct((b, t, d), F32),
                      jax.ShapeDtypeStruct((b * t, n_exp), F32)]
    return pl.pallas_call(
        functools.partial(_out_kernel, tm=tm, n_ctx=n_ctx, alpha=alpha, moe=moe),
        grid=(b, t // tm),
        in_specs=in_specs,
        out_specs=out_specs,
        out_shape=out_shape,
        compiler_params=_cparams(("arbitrary", "arbitrary")),
        name="out_proj_moe" if moe else "out_proj",
    )(*args)


def _swiglu(x, w1_ref, w3_ref, w2_ref, fc):
    f = w1_ref.shape[2]
    acc = None
    for c in range(f // fc):
        cs = slice(c * fc, (c + 1) * fc)
        a = jnp.dot(x, w1_ref[0, :, cs], preferred_element_type=F32)
        bgate = jnp.dot(x, w3_ref[0, :, cs], preferred_element_type=F32)
        gact = (a * _sigmoid(a) * bgate).astype(BF16)
        part = jnp.dot(gact, w2_ref[0, cs, :], preferred_element_type=F32)
        acc = part if acc is None else acc + part
    return acc


def _ffn_chunk(f):
    return MXU_DIM if f % MXU_DIM == 0 else f


def _dense_ffn_kernel(h1_ref, mb_ref, mc_ref, w1_ref, w3_ref, w2_ref, g_ref, b_ref, o_ref,
                      *, tf, n_ctx, alpha, fc):
    row = pl.program_id(1) * tf + lax.broadcasted_iota(jnp.int32, (tf, 1), 0)
    is_ctx = row < n_ctx
    sh2 = _row_mods(mb_ref, mc_ref, 3, is_ctx)
    sc2 = _row_mods(mb_ref, mc_ref, 4, is_ctx)
    g2 = _row_mods(mb_ref, mc_ref, 5, is_ctx)
    h1 = h1_ref[0]
    u = (h1 * (1.0 + sc2) + sh2).astype(BF16)
    mixed = _swiglu(u, w1_ref, w3_ref, w2_ref, fc)
    o_ref[0] = _layer_norm(alpha * h1 + (1.0 + g2) * mixed, g_ref[...], b_ref[...])


def _dense_ffn(h1, mod, w1, w3, w2, ln_g, ln_b, n_ctx, tf, alpha):
    b, t, d = h1.shape
    f = w1.shape[2]
    row_spec = pl.BlockSpec((1, tf, d), lambda i, j: (i, j, 0))
    vec = pl.BlockSpec((1, d), lambda i, j: (0, 0))
    const3 = lambda a: pl.BlockSpec(a.shape, lambda i, j: (0, 0, 0))
    return pl.pallas_call(
        functools.partial(_dense_ffn_kernel, tf=tf, n_ctx=n_ctx, alpha=alpha, fc=_ffn_chunk(f)),
        grid=(b, t // tf),
        in_specs=[
            row_spec,
            pl.BlockSpec((1, 6, d), lambda i, j: (i, 0, 0)),
            pl.BlockSpec((1, 6, d), lambda i, j: (b, 0, 0)),
            const3(w1), const3(w3), const3(w2), vec, vec,
        ],
        out_specs=row_spec,
        out_shape=jax.ShapeDtypeStruct((b, t, d), F32),
        compiler_params=_cparams(("arbitrary", "arbitrary")),
        name="dense_ffn",
    )(h1, mod, mod, w1, w3, w2, ln_g, ln_b)


def _moe_ffn_kernel(te_ref, nu_ref, src_ref, nxt_ref, u_ref, w1_ref, w3_ref, w2_ref, o_ref,
                    xbuf, sem, *, tm, fc):
    del te_ref
    i = pl.program_id(0)
    n_used = nu_ref[0]
    slot = i % 2

    def gather(idx_ref, dst_slot):
        for r in range(tm):
            pltpu.make_async_copy(u_ref.at[pl.ds(idx_ref[0, 0, r], 1)],
                                  xbuf.at[dst_slot, pl.ds(r, 1)], sem.at[dst_slot]).start()

    @pl.when(jnp.logical_and(i == 0, n_used > 0))
    def _():
        gather(src_ref, 0)

    @pl.when(i + 1 < n_used)
    def _():
        gather(nxt_ref, 1 - slot)

    @pl.when(i < n_used)
    def _():
        pltpu.make_async_copy(u_ref.at[pl.ds(0, tm)], xbuf.at[slot], sem.at[slot]).wait()
        o_ref[...] = _swiglu(xbuf[slot].astype(BF16), w1_ref, w3_ref, w2_ref, fc)

    @pl.when(i >= n_used)
    def _():
        o_ref[...] = jnp.zeros_like(o_ref)


def _moe_ffn(u, src, w1, w3, w2, tile_expert, n_used, tm):
    n, d = u.shape
    f = w1.shape[2]
    n_tiles = src.shape[0] // tm
    src3 = src.reshape(n_tiles, 1, tm)
    wspec = lambda shape: pl.BlockSpec(shape, lambda i, te, nu: (te[i], 0, 0))
    grid_spec = pltpu.PrefetchScalarGridSpec(
        num_scalar_prefetch=2,
        grid=(n_tiles,),
        in_specs=[
            pl.BlockSpec((1, 1, tm), lambda i, te, nu: (i, 0, 0), memory_space=pltpu.SMEM),
            pl.BlockSpec((1, 1, tm), lambda i, te, nu: (jnp.minimum(i + 1, n_tiles - 1), 0, 0),
                         memory_space=pltpu.SMEM),
            pl.BlockSpec(memory_space=pl.ANY),
            wspec((1, d, f)), wspec((1, d, f)), wspec((1, f, d)),
        ],
        out_specs=pl.BlockSpec((tm, d), lambda i, te, nu: (i, 0)),
        scratch_shapes=[pltpu.VMEM((2, tm, d), F32), pltpu.SemaphoreType.DMA((2,))],
    )
    return pl.pallas_call(
        functools.partial(_moe_ffn_kernel, tm=tm, fc=_ffn_chunk(f)),
        grid_spec=grid_spec,
        out_shape=jax.ShapeDtypeStruct((n_tiles * tm, d), F32),
        compiler_params=_cparams(("arbitrary",)),
        name="moe_ffn",
    )(tile_expert, n_used, src3, src3, u, w1, w3, w2)


def _combine_kernel(dst_ref, nxt_ref, ys_ref, route_ref, h1_ref, mb_ref, mc_ref, g_ref, b_ref,
                    o_ref, buf, sem, *, tc, n_ctx, first_tile, alpha):
    i, j = pl.program_id(0), pl.program_id(1)
    nj = pl.num_programs(1)
    s = i * nj + j
    total = pl.num_programs(0) * nj
    slot = s % 2

    def gather(idx_ref, dslot):
        for r in range(tc):
            for k in range(N_TOP):
                pltpu.make_async_copy(ys_ref.at[pl.ds(idx_ref[0, 0, N_TOP * r + k], 1)],
                                      buf.at[dslot, pl.ds(k * tc + r, 1)], sem.at[dslot]).start()

    @pl.when(s == 0)
    def _():
        gather(dst_ref, 0)

    @pl.when(s + 1 < total)
    def _():
        gather(nxt_ref, 1 - slot)

    pltpu.make_async_copy(ys_ref.at[pl.ds(0, N_TOP * tc)], buf.at[slot], sem.at[slot]).wait()
    rt = route_ref[...]
    mixed = rt[:, 2:3] * buf[slot, 0:tc] + rt[:, 3:4] * buf[slot, tc:N_TOP * tc]
    row = (first_tile + j) * tc + lax.broadcasted_iota(jnp.int32, (tc, 1), 0)
    g2 = _row_mods(mb_ref, mc_ref, 5, row < n_ctx)
    o_ref[0] = _layer_norm(alpha * h1_ref[0] + (1.0 + g2) * mixed, g_ref[...], b_ref[...])


def _moe_combine(h1, mod, ln_g, ln_b, ys, dst, route, n_ctx, tc, alpha, skip_ctx):
    b, t, d = h1.shape
    tiles = t // tc
    off = n_ctx // tc if skip_ctx else 0
    nj = tiles - off
    n_exp = route.shape[1]
    dst3 = dst.reshape(b * tiles, 1, N_TOP * tc)

    def cur(i, j):
        return i * tiles + off + j

    def nxt(i, j):
        i2 = jnp.minimum(i + (j + 1) // nj, b - 1)
        return i2 * tiles + off + (j + 1) % nj

    vec = pl.BlockSpec((1, d), lambda i, j: (0, 0))
    return pl.pallas_call(
        functools.partial(_combine_kernel, tc=tc, n_ctx=n_ctx, first_tile=off, alpha=alpha),
        grid=(b, nj),
        in_specs=[
            pl.BlockSpec((1, 1, N_TOP * tc), lambda i, j: (cur(i, j), 0, 0),
                         memory_space=pltpu.SMEM),
            pl.BlockSpec((1, 1, N_TOP * tc), lambda i, j: (nxt(i, j), 0, 0),
                         memory_space=pltpu.SMEM),
            pl.BlockSpec(memory_space=pl.ANY),
            pl.BlockSpec((tc, n_exp), lambda i, j: (cur(i, j), 0)),
            pl.BlockSpec((1, tc, d), lambda i, j: (i, off + j, 0)),
            pl.BlockSpec((1, 6, d), lambda i, j: (i, 0, 0)),
            pl.BlockSpec((1, 6, d), lambda i, j: (b, 0, 0)),
            vec, vec,
        ],
        out_specs=pl.BlockSpec((1, tc, d), lambda i, j: (i, j, 0)),
        out_shape=jax.ShapeDtypeStruct((b, nj * tc, d), F32),
        scratch_shapes=[pltpu.VMEM((2, N_TOP * tc, d), F32), pltpu.SemaphoreType.DMA((2,))],
        compiler_params=_cparams(("arbitrary", "arbitrary")),
        name="moe_combine",
    )(dst3, dst3, ys, route, h1, mod, mod, ln_g, ln_b)


def _rope_tables(n_ctx, seq):
    rows = seq // GRID_W
    pairs = HEAD_DIM // 4
    rpos = jnp.repeat(jnp.arange(rows, dtype=F32), GRID_W)
    cpos = jnp.tile(jnp.arange(GRID_W, dtype=F32), rows)
    inv_freq = ROPE_BASE ** (-jnp.arange(pairs, dtype=F32) / pairs)
    ang = jnp.concatenate([rpos[:, None] * inv_freq, cpos[:, None] * inv_freq], -1)
    cos, sin = jnp.cos(ang), jnp.sin(ang)
    reps = LANES // HEAD_DIM
    cs = jnp.tile(jnp.concatenate([cos, cos], -1), (1, reps))
    sn = jnp.tile(jnp.concatenate([-sin, sin], -1), (1, reps))
    cs = jnp.concatenate([jnp.ones((n_ctx, LANES), F32), cs], 0)
    sn = jnp.concatenate([jnp.zeros((n_ctx, LANES), F32), sn], 0)
    return cs, sn


def _prep_w_in(w_in):
    lead = w_in.shape[:-1]
    q0 = 3 * ATTN_W

    def split_pairs(w):
        w = w.reshape(*lead, ATTN_W // HEAD_DIM, HEAD_DIM // 2, 2)
        return jnp.swapaxes(w, -1, -2).reshape(*lead, ATTN_W)

    q_scale = HEAD_DIM ** -0.5 * math.log2(math.e)
    return jnp.concatenate([
        split_pairs(w_in[..., :ATTN_W]), w_in[..., ATTN_W:q0],
        split_pairs(w_in[..., q0:q0 + ATTN_W]) * q_scale, w_in[..., q0 + ATTN_W:]],
        axis=-1).astype(BF16)


def _prep_rg_gates(wa, wx):
    nd, g, bw, _ = wa.shape
    per = MXU_DIM // bw
    halves = g // per

    def bdiag(w):
        w = w.reshape(nd, halves, per, bw, bw)
        eye = jnp.eye(per, dtype=w.dtype)
        full = w[:, :, :, :, None, :] * eye[None, None, :, None, :, None]
        return full.reshape(nd, halves, per * bw, per * bw)

    return (0.5 * jnp.concatenate([bdiag(wa), bdiag(wx)], -1)).astype(BF16)


def _routing_tables(route, keep, n_exp, tm):
    e = route[:, :N_TOP].astype(jnp.int32).reshape(-1)
    keep2 = jnp.repeat(keep, N_TOP)
    oh = ((e[:, None] == jnp.arange(n_exp)[None, :]) & keep2[:, None]).astype(jnp.int32)
    csum = jnp.cumsum(oh, axis=0)
    rank = jnp.sum((csum - oh) * oh, axis=1)
    counts = csum[-1]
    tiles_per = (counts + tm - 1) // tm
    tile_end = jnp.cumsum(tiles_per)
    off = (tile_end - tiles_per) * tm
    n_tiles = e.shape[0] // tm + n_exp
    rows = n_tiles * tm
    dst = jnp.where(keep2, jnp.sum(oh * off[None, :], axis=1) + rank, rows).astype(jnp.int32)
    tile_ids = jnp.arange(n_tiles)
    tile_expert = jnp.sum((tile_ids[:, None] >= tile_end[None, :]).astype(jnp.int32), axis=1)
    tile_expert = jnp.minimum(tile_expert, n_exp - 1).astype(jnp.int32)
    order = jnp.argsort(jnp.where(keep2, e, n_exp), stable=True).astype(jnp.int32)
    row_expert = jnp.repeat(tile_expert, tm)
    k_in = jnp.arange(rows, dtype=jnp.int32) - off[row_expert]
    first = jnp.cumsum(counts) - counts
    slot = jnp.clip(first[row_expert] + k_in, 0, e.shape[0] - 1)
    src = jnp.where(k_in < counts[row_expert], order[slot] // N_TOP, 0).astype(jnp.int32)
    return dst, src, tile_expert, tile_end[-1:].astype(jnp.int32)


def kernel(x, c, ctx, c_ctx, w_mod, b_mod, w_in, lam_q1, lam_k1, lam_q2, lam_k2, subln_g,
           conv_w, conv_b, rg_wa, rg_ba, rg_wx, rg_bx, rg_lambda, w_out,
           ln1_g, ln1_b, ln2_g, ln2_b, ffn_w1, ffn_w3, ffn_w2,
           moe_router, moe_w1, moe_w3, moe_w2):
    b, seq, d = x.shape
    n_ctx = ctx.shape[1]
    t = n_ctx + seq
    depth = w_in.shape[0]
    alpha = (2 * depth) ** 0.25
    assert seq % GRID_W == 0 and d - ATTN_W == ATTN_W and t % n_ctx == 0

    tm = _pick_tile(t, 768)
    tf = _pick_tile(t, 768)
    te = 512 if (N_TOP * b * t) % 512 == 0 else _pick_tile(N_TOP * b * t, 512)
    tc = _pick_tile(n_ctx, 256)
    chunk = math.gcd(n_ctx, 256)

    h = jnp.concatenate([ctx, x], axis=1)
    mrows = -(-(b + 1) // SUBLANES) * SUBLANES
    cvec = jnp.zeros((mrows, d), F32).at[:b].set(c).at[b].set(c_ctx)
    mod_all = _modulation(cvec, w_mod, b_mod).reshape(depth, mrows, 6, d)
    cs_tab, sn_tab = _rope_tables(n_ctx, seq)

    w_in_p = _prep_w_in(w_in)
    w_out_b = w_out.astype(BF16)
    lamv = jnp.stack([lam_q1, lam_k1, lam_q2, lam_k2], axis=1)
    rg_bias = jnp.stack([rg_ba, rg_bx], axis=2).reshape(depth, 4, -1)
    is_lat = jnp.tile(jnp.arange(t) >= n_ctx, b)

    for i in range(depth):
        last = i == depth - 1
        lam_init = 0.8 - 0.6 * math.exp(-0.3 * i)
        mod = mod_all[i]
        g2_ln = (ln2_g[i][None, :], ln2_b[i][None, :])
        k, v, xr, q, y = _in_proj(h, mod, w_in_p[i], cs_tab, sn_tab, n_ctx, tm)
        att = _attention(q, k, v, lamv[i], subln_g[i][None, :], n_ctx, lam_init)
        rg = _rglru(xr, y, conv_w[i], conv_b[i][None, :], _prep_rg_gates(rg_wa[i], rg_wx[i]),
                    rg_bias[i], rg_lambda[i], n_ctx, chunk)
        j = i // 2
        router = moe_router[j] if i % 2 else None
        outs = _out_proj(att, rg, h, mod, w_out_b[i], ln1_g[i][None, :], ln1_b[i][None, :],
                         router, n_ctx, tm, alpha)
        h1 = outs[0]
        if i % 2 == 0:
            h = _dense_ffn(h1, mod, ffn_w1[j][None].astype(BF16), ffn_w3[j][None].astype(BF16),
                           ffn_w2[j][None].astype(BF16), *g2_ln, n_ctx, tf, alpha)
        else:
            u, route = outs[1], outs[2]
            n_exp = route.shape[1]
            keep = is_lat if last else jnp.ones((b * t,), bool)
            dst, src, tile_expert, n_used = _routing_tables(route, keep, n_exp, te)
            ys = _moe_ffn(u.reshape(b * t, d), src, moe_w1[j].astype(BF16),
                          moe_w3[j].astype(BF16), moe_w2[j].astype(BF16), tile_expert, n_used, te)
            h = _moe_combine(h1, mod, *g2_ln, ys, dst, route, n_ctx, tc, alpha, skip_ctx=last)
    return h if h.shape[1] == seq else h[:, n_ctx:, :]
```

```python
import functools
import math

import jax
import jax.numpy as jnp
from jax import lax
from jax.experimental import pallas as pl
from jax.experimental.pallas import tpu as pltpu

F32 = jnp.float32
BF16 = jnp.bfloat16

HEAD_DIM = 64
ATTN_HEADS = 4
HEAD_W = 2 * HEAD_DIM
ATTN_W = ATTN_HEADS * HEAD_W
GRID_W = 64
RG_BLOCKS = 8
RG_C = 8.0
CONV_W = 4
N_TOP = 2
ROPE_BASE = 10000.0
LN_EPS = 1e-5
RMS_EPS = 1e-5

LANES = 128
SUBLANES = 8
MXU_DIM = 256
VMEM_LIMIT = 56 * 1024 * 1024


def _cparams(sem, vmem=VMEM_LIMIT):
    return pltpu.CompilerParams(dimension_semantics=sem, vmem_limit_bytes=vmem)


def _pick_tile(n, pref):
    t = min(n, pref)
    while t > 8 and (n % t or t % 8):
        t -= 8
    assert n % t == 0
    return t


def _layer_norm(z, g, b):
    mu = jnp.mean(z, -1, keepdims=True)
    zc = z - mu
    var = jnp.mean(zc * zc, -1, keepdims=True)
    return zc * lax.rsqrt(var + LN_EPS) * g + b


def _row_mods(mb_ref, mc_ref, idx, is_ctx):
    return jnp.where(is_ctx, mc_ref[0, idx:idx + 1, :], mb_ref[0, idx:idx + 1, :])


def _sigmoid(x):
    return 0.5 * jnp.tanh(0.5 * x) + 0.5


def _mod_kernel(c_ref, w_ref, b_ref, o_ref):
    cv = c_ref[...]
    s = (cv * _sigmoid(cv)).astype(BF16)
    o_ref[0] = jnp.dot(s, w_ref[0].astype(BF16), preferred_element_type=F32) + b_ref[0]


def _modulation(cvec, w_mod, b_mod):
    depth, d, n = w_mod.shape
    mb = cvec.shape[0]
    tn = _pick_tile(n, 1536)
    return pl.pallas_call(
        _mod_kernel,
        grid=(depth, n // tn),
        in_specs=[
            pl.BlockSpec((mb, d), lambda i, j: (0, 0)),
            pl.BlockSpec((1, d, tn), lambda i, j: (i, 0, j)),
            pl.BlockSpec((1, 1, tn), lambda i, j: (i, 0, j)),
        ],
        out_specs=pl.BlockSpec((1, mb, tn), lambda i, j: (i, 0, j)),
        out_shape=jax.ShapeDtypeStruct((depth, mb, n), F32),
        compiler_params=_cparams(("arbitrary", "arbitrary")),
        name="adaln_mod",
    )(cvec, w_mod, b_mod.reshape(depth, 1, n))


def _rope_store(t, cs, sn, out_ref):
    lane = lax.broadcasted_iota(jnp.int32, (1, LANES), 1)
    first = (lane % HEAD_DIM) < (HEAD_DIM // 2)
    for g in range(ATTN_W // LANES):
        tg = t[:, g * LANES:(g + 1) * LANES]
        sw = jnp.where(first, pltpu.roll(tg, LANES - HEAD_DIM // 2, 1),
                       pltpu.roll(tg, HEAD_DIM // 2, 1))
        out_ref[0, :, g * LANES:(g + 1) * LANES] = (tg * cs + sw * sn).astype(out_ref.dtype)


def _in_kernel(h_ref, mb_ref, mc_ref, w_ref, cs_ref, sn_ref,
               k_ref, v_ref, xr_ref, q_ref, y_ref, *, tm, n_ctx):
    j = pl.program_id(1)
    row = j * tm + lax.broadcasted_iota(jnp.int32, (tm, 1), 0)
    is_ctx = row < n_ctx
    shift = _row_mods(mb_ref, mc_ref, 0, is_ctx)
    scale = _row_mods(mb_ref, mc_ref, 1, is_ctx)
    x = (h_ref[0] * (1.0 + scale) + shift).astype(BF16)
    cs = cs_ref[...]
    sn = sn_ref[...]
    w = ATTN_W

    def proj(part):
        return jnp.dot(x, w_ref[:, part * w:(part + 1) * w], preferred_element_type=F32)

    _rope_store(proj(0), cs, sn, k_ref)
    v_ref[0] = proj(1).astype(v_ref.dtype)
    xr_ref[0] = proj(2)
    _rope_store(proj(3), cs, sn, q_ref)
    y_ref[0] = proj(4)


def _in_proj(h, mod, w_in, cs_tab, sn_tab, n_ctx, tm):
    b, t, d = h.shape
    n = w_in.shape[1]
    w = ATTN_W
    assert n == 5 * w
    row_spec = lambda width: pl.BlockSpec((1, tm, width), lambda i, j: (i, j, 0))
    return pl.pallas_call(
        functools.partial(_in_kernel, tm=tm, n_ctx=n_ctx),
        grid=(b, t // tm),
        in_specs=[
            row_spec(d),
            pl.BlockSpec((1, 6, d), lambda i, j: (i, 0, 0)),
            pl.BlockSpec((1, 6, d), lambda i, j: (b, 0, 0)),
            pl.BlockSpec((d, n), lambda i, j: (0, 0)),
            pl.BlockSpec((tm, LANES), lambda i, j: (j, 0)),
            pl.BlockSpec((tm, LANES), lambda i, j: (j, 0)),
        ],
        out_specs=[row_spec(w)] * 5,
        out_shape=[
            jax.ShapeDtypeStruct((b, t, w), BF16),
            jax.ShapeDtypeStruct((b, t, w), BF16),
            jax.ShapeDtypeStruct((b, t, w), F32),
            jax.ShapeDtypeStruct((b, t, w), BF16),
            jax.ShapeDtypeStruct((b, t, w), F32),
        ],
        compiler_params=_cparams(("arbitrary", "arbitrary")),
        name="in_proj",
    )(h, mod, mod, w_in, cs_tab, sn_tab)


def _attn_kernel(q_ref, k_ref, v_ref, lamv_ref, g_ref, o_ref, *, n_ctx, n_all, lam_init):
    j = pl.program_id(1)
    lv = lamv_ref[...]
    lam = (jnp.exp(jnp.sum(lv[0:1] * lv[1:2], keepdims=True))
           - jnp.exp(jnp.sum(lv[2:3] * lv[3:4], keepdims=True)) + lam_init)
    gain = g_ref[...] * (1.0 - lam_init)
    lane = lax.broadcasted_iota(jnp.int32, (1, HEAD_W), 1)
    dn = (((1,), (1,)), ((), ()))

    def softmax_av(qm, kh, vext):
        s = lax.dot_general(qm, kh, dn, preferred_element_type=F32)
        p = jnp.exp2(s - jnp.max(s, -1, keepdims=True)).astype(BF16)
        oe = jnp.dot(p, vext, preferred_element_type=F32)
        return oe[:, :HEAD_W] / oe[:, HEAD_W:HEAD_W + 1]

    def run(nk):
        ones = jnp.ones((nk, HEAD_W), BF16)
        for h in range(ATTN_HEADS):
            sl = slice(h * HEAD_W, (h + 1) * HEAD_W)
            qh = q_ref[0, :, sl]
            kh = k_ref[0, 0:nk, sl]
            vext = jnp.concatenate([v_ref[0, 0:nk, sl], ones], axis=1)
            q1 = jnp.where(lane < HEAD_DIM, qh, jnp.zeros_like(qh))
            q2 = jnp.where(lane >= HEAD_DIM, qh, jnp.zeros_like(qh))
            o = softmax_av(q1, kh, vext) - lam * softmax_av(q2, kh, vext)
            o = o * lax.rsqrt(jnp.mean(o * o, -1, keepdims=True) + RMS_EPS) * gain
            o_ref[0, :, sl] = o.astype(o_ref.dtype)

    def run_small(nk):
        ones = jnp.ones((nk, HEAD_W), BF16)
        sls = [slice(h * HEAD_W, (h + 1) * HEAD_W) for h in range(ATTN_HEADS)]
        masks = (lane < HEAD_DIM, lane >= HEAD_DIM)
        ss = []
        for sl in sls:
            qh = q_ref[0, :, sl]
            kh = k_ref[0, 0:nk, sl]
            for mk in masks:
                ss.append(lax.dot_general(jnp.where(mk, qh, jnp.zeros_like(qh)), kh, dn,
                                          preferred_element_type=F32))
        mx = [jnp.max(s, -1, keepdims=True) for s in ss]
        ps = [jnp.exp2(s - m).astype(BF16) for s, m in zip(ss, mx)]
        vexts = [jnp.concatenate([v_ref[0, 0:nk, sl], ones], axis=1) for sl in sls]
        oes = [jnp.dot(p, vexts[n // 2], preferred_element_type=F32) for n, p in enumerate(ps)]
        ons = [oe[:, :HEAD_W] / oe[:, HEAD_W:HEAD_W + 1] for oe in oes]
        os_ = [ons[2 * h] - lam * ons[2 * h + 1] for h in range(ATTN_HEADS)]
        ms = [jnp.mean(o * o, -1, keepdims=True) for o in os_]
        for h, sl in enumerate(sls):
            o_ref[0, :, sl] = (os_[h] * lax.rsqrt(ms[h] + RMS_EPS) * gain).astype(o_ref.dtype)

    @pl.when(j == 0)
    def _():
        run_small(n_ctx)

    @pl.when(j > 0)
    def _():
        run(n_all)


def _attention(q, k, v, lamv, gain, n_ctx, lam_init):
    b, t, w = q.shape
    tq = n_ctx
    return pl.pallas_call(
        functools.partial(_attn_kernel, n_ctx=n_ctx, n_all=t, lam_init=lam_init),
        grid=(b, t // tq),
        in_specs=[
            pl.BlockSpec((1, tq, w), lambda i, j: (i, j, 0)),
            pl.BlockSpec((1, t, w), lambda i, j: (i, 0, 0)),
            pl.BlockSpec((1, t, w), lambda i, j: (i, 0, 0)),
            pl.BlockSpec((4, HEAD_DIM), lambda i, j: (0, 0)),
            pl.BlockSpec((1, HEAD_W), lambda i, j: (0, 0)),
        ],
        out_specs=pl.BlockSpec((1, tq, w), lambda i, j: (i, j, 0)),
        out_shape=jax.ShapeDtypeStruct((b, t, w), BF16),
        compiler_params=_cparams(("arbitrary", "arbitrary")),
        name="diff_attn",
    )(q, k, v, lamv, gain)


def _gelu_tanh(x):
    return 0.5 * x * (1.0 + jnp.tanh(math.sqrt(2.0 / math.pi) * (x + 0.044715 * x * x * x)))


def _softplus(z):
    return jnp.maximum(z, 0.0) + jnp.log1p(jnp.exp(-jnp.abs(z)))


def _rg_kernel(xr_ref, y_ref, cw_ref, cb_ref, wg_ref, bias_ref, lam_ref, o_ref,
               xc_sc, acc_sc, a_sl, b_sl, *, n_ctx, n_all, chunk):
    t, c, r = n_all, n_ctx, chunk
    half = MXU_DIM
    rgw = xc_sc.shape[1]
    groups = rgw // LANES
    nseg = SUBLANES
    seg = r // nseg

    row = lax.broadcasted_iota(jnp.int32, (t, 1), 0)
    in_ctx = row < c
    tl = jnp.where(in_ctx, row, row - c)
    span = jnp.where(in_ctx, c, t - c)
    for g in range(groups):
        sl = slice(g * LANES, (g + 1) * LANES)
        xg = xr_ref[0, :, sl]
        wv = cw_ref[:, sl]
        acc = cb_ref[:, sl] + xg * wv[2:3]
        acc = acc + jnp.where(tl >= 2, pltpu.roll(xg, 2, 0), 0.0) * wv[0:1]
        acc = acc + jnp.where(tl >= 1, pltpu.roll(xg, 1, 0), 0.0) * wv[1:2]
        acc = acc + jnp.where(tl + 1 < span, pltpu.roll(xg, t - 1, 0), 0.0) * wv[3:4]
        xc_sc[:, sl] = acc

    n_chunks = t // r
    ctx_chunks = c // r
    for d in range(2):
        c4 = (-0.5 * RG_C) * _softplus(-lam_ref[d:d + 1, :])
        ba = 0.5 * bias_ref[2 * d:2 * d + 1, :]
        bx = 0.5 * bias_ref[2 * d + 1:2 * d + 2, :]

        def chunk_step(s, hc, d=d, c4=c4, ba=ba, bx=bx):
            if d == 0:
                ci = s
            else:
                ci = jnp.where(s < ctx_chunks, ctx_chunks - 1 - s,
                               n_chunks - 1 - (s - ctx_chunks))
            r0 = pl.multiple_of(ci * r, r)
            xc = xc_sc[pl.ds(r0, r), :]
            xb = xc.astype(BF16)
            for hh in range(rgw // half):
                hs = slice(hh * half, (hh + 1) * half)
                g2 = jnp.dot(xb[:, hs], wg_ref[d, hh], preferred_element_type=F32)
                tr = jnp.tanh(g2[:, :half] + ba[:, hs])
                gi = 0.5 * jnp.tanh(g2[:, half:] + bx[:, hs]) + 0.5
                log_a = c4[:, hs] * (tr + 1.0)
                a = jnp.exp(log_a)
                th = jnp.tanh(log_a)
                bt = jnp.sqrt(-2.0 * th) * lax.rsqrt(1.0 - th) * (gi * xc[:, hs])
                for gg in range(half // LANES):
                    g = hh * (half // LANES) + gg
                    ls = slice(gg * LANES, (gg + 1) * LANES)
                    for k in range(nseg):
                        rows = slice(k * seg, (k + 1) * seg)
                        a_sl[g, pl.ds(k, seg, stride=nseg), :] = a[rows, ls]
                        b_sl[g, pl.ds(k, seg, stride=nseg), :] = bt[rows, ls]

            def seg_step(ii, carry):
                i = ii if d == 0 else seg - 1 - ii
                base = pl.multiple_of(i * nseg, nseg)
                hs_, as_ = carry
                nh, na = [], []
                for g in range(groups):
                    av = a_sl[g, pl.ds(base, nseg), :]
                    hv = av * hs_[g] + b_sl[g, pl.ds(base, nseg), :]
                    pv = av * as_[g]
                    b_sl[g, pl.ds(base, nseg), :] = hv
                    a_sl[g, pl.ds(base, nseg), :] = pv
                    nh.append(hv)
                    na.append(pv)
                return tuple(nh), tuple(na)

            zero = jnp.zeros((nseg, LANES), F32)
            one = jnp.ones((nseg, LANES), F32)
            h_end, a_end = lax.fori_loop(
                0, seg, seg_step, ((zero,) * groups, (one,) * groups), unroll=4)

            order = range(nseg) if d == 0 else range(nseg - 1, -1, -1)
            new_carry = []
            for g in range(groups):
                ls = slice(g * LANES, (g + 1) * LANES)
                h_in = hc[:, ls]
                for k in order:
                    rows = pl.ds(r0 + k * seg, seg)
                    h_true = (b_sl[g, pl.ds(k, seg, stride=nseg), :]
                              + a_sl[g, pl.ds(k, seg, stride=nseg), :] * h_in)
                    if d == 0:
                        acc_sc[rows, ls] = h_true
                    else:
                        tot = acc_sc[rows, ls] + h_true
                        o_ref[0, rows, ls] = (tot * _gelu_tanh(y_ref[0, rows, ls])).astype(o_ref.dtype)
                    h_in = h_end[g][k:k + 1, :] + a_end[g][k:k + 1, :] * h_in
                new_carry.append(h_in)
            return jnp.concatenate(new_carry, axis=1)

        lax.fori_loop(0, n_chunks, chunk_step, jnp.zeros((1, rgw), F32))


def _rglru(xr, y, conv_w, conv_b, wg, bias, lam, n_ctx, chunk):
    b, t, w = xr.shape
    full = lambda a: pl.BlockSpec(a.shape, lambda i: (0,) * a.ndim)
    seq = pl.BlockSpec((1, t, w), lambda i: (i, 0, 0))
    return pl.pallas_call(
        functools.partial(_rg_kernel, n_ctx=n_ctx, n_all=t, chunk=chunk),
        grid=(b,),
        in_specs=[seq, seq, full(conv_w), full(conv_b), full(wg), full(bias), full(lam)],
        out_specs=seq,
        out_shape=jax.ShapeDtypeStruct((b, t, w), BF16),
        scratch_shapes=[
            pltpu.VMEM((t, w), F32),
            pltpu.VMEM((t, w), F32),
            pltpu.VMEM((w // LANES, chunk, LANES), F32),
            pltpu.VMEM((w // LANES, chunk, LANES), F32),
        ],
        compiler_params=_cparams(("arbitrary",)),
        name="rglru",
    )(xr, y, conv_w, conv_b, wg, bias, lam)


def _out_kernel(*refs, tm, n_ctx, alpha, moe):
    if moe:
        (att_ref, rg_ref, h_ref, mb_ref, mc_ref, w_ref, g_ref, b_ref, rt_ref,
         h1_ref, u_ref, route_ref) = refs
    else:
        att_ref, rg_ref, h_ref, mb_ref, mc_ref, w_ref, g_ref, b_ref, h1_ref = refs
    j = pl.program_id(1)
    row = j * tm + lax.broadcasted_iota(jnp.int32, (tm, 1), 0)
    is_ctx = row < n_ctx
    aw = att_ref.shape[2]
    mix = (jnp.dot(att_ref[0], w_ref[0:aw, :], preferred_element_type=F32)
           + jnp.dot(rg_ref[0], w_ref[aw:, :], preferred_element_type=F32))
    g1 = _row_mods(mb_ref, mc_ref, 2, is_ctx)
    h1 = _layer_norm(alpha * h_ref[0] + (1.0 + g1) * mix, g_ref[...], b_ref[...])
    h1_ref[0] = h1
    if not moe:
        return
    sh2 = _row_mods(mb_ref, mc_ref, 3, is_ctx)
    sc2 = _row_mods(mb_ref, mc_ref, 4, is_ctx)
    u = h1 * (1.0 + sc2) + sh2
    u_ref[0] = u
    n_exp = route_ref.shape[1]
    u_hi = u.astype(BF16)
    u_lo = (u - u_hi.astype(F32)).astype(BF16)
    prod = (jnp.dot(u_hi, rt_ref[...], preferred_element_type=F32)
            + jnp.dot(u_lo, rt_ref[...], preferred_element_type=F32))
    logits = prod[:, :LANES] + prod[:, LANES:]
    lane = lax.broadcasted_iota(jnp.int32, (1, LANES), 1)
    neg = -jnp.inf
    lg = jnp.where(lane < n_exp, logits, neg)
    m1 = jnp.max(lg, -1, keepdims=True)
    i1 = jnp.min(jnp.where(lg == m1, lane, LANES), -1, keepdims=True)
    lg2 = jnp.where(lane == i1, neg, lg)
    m2 = jnp.max(lg2, -1, keepdims=True)
    i2 = jnp.min(jnp.where(lg2 == m2, lane, LANES), -1, keepdims=True)
    e2 = jnp.exp(m2 - m1)
    den = 1.0 + e2
    route = jnp.where(lane == 0, i1.astype(F32),
                      jnp.where(lane == 1, i2.astype(F32),
                                jnp.where(lane == 2, 1.0 / den,
                                          jnp.where(lane == 3, e2 / den, 0.0))))
    route_ref[...] = route[:, :n_exp]


def _out_proj(att, rg, h, mod, w_out, ln_g, ln_b, router, n_ctx, tm, alpha):
    b, t, d = h.shape
    aw = att.shape[2]
    moe = router is not None
    row_spec = lambda width: pl.BlockSpec((1, tm, width), lambda i, j: (i, j, 0))
    vec = pl.BlockSpec((1, d), lambda i, j: (0, 0))
    in_specs = [
        row_spec(aw), row_spec(aw), row_spec(d),
        pl.BlockSpec((1, 6, d), lambda i, j: (i, 0, 0)),
        pl.BlockSpec((1, 6, d), lambda i, j: (b, 0, 0)),
        pl.BlockSpec((d, d), lambda i, j: (0, 0)),
        vec, vec,
    ]
    args = [att, rg, h, mod, mod, w_out, ln_g, ln_b]
    out_specs = [row_spec(d)]
    out_shape = [jax.ShapeDtypeStruct((b, t, d), F32)]
    if moe:
        n_exp = router.shape[1]
        r_hi = router.astype(BF16)
        r_lo = (router - r_hi.astype(F32)).astype(BF16)
        pad = lambda a: jnp.zeros((d, LANES), BF16).at[:, :n_exp].set(a)
        in_specs.append(pl.BlockSpec((d, 2 * LANES), lambda i, j: (0, 0)))
        args.append(jnp.concatenate([pad(r_hi), pad(r_lo)], axis=1))
        tiles = t // tm
        out_specs += [row_spec(d), pl.BlockSpec((tm, n_exp), lambda i, j: (i * tiles + j, 0))]
        out_shape += [jax.ShapeDtypeStruct((b, t, d), F32),
                      jax.ShapeDtypeStruct((b * t, n_exp), F32)]
    return pl.pallas_call(
        functools.partial(_out_kernel, tm=tm, n_ctx=n_ctx, alpha=alpha, moe=moe),
        grid=(b, t // tm),
        in_specs=in_specs,
        out_specs=out_specs,
        out_shape=out_shape,
        compiler_params=_cparams(("arbitrary", "arbitrary")),
        name="out_proj_moe" if moe else "out_proj",
    )(*args)


def _swiglu(x, w1_ref, w3_ref, w2_ref, fc):
    f = w1_ref.shape[2]
    acc = None
    for c in range(f // fc):
        cs = slice(c * fc, (c + 1) * fc)
        a = jnp.dot(x, w1_ref[0, :, cs], preferred_element_type=F32)
        bgate = jnp.dot(x, w3_ref[0, :, cs], preferred_element_type=F32)
        gact = (a * _sigmoid(a) * bgate).astype(BF16)
        part = jnp.dot(gact, w2_ref[0, cs, :], preferred_element_type=F32)
        acc = part if acc is None else acc + part
    return acc


def _ffn_chunk(f):
    return MXU_DIM if f % MXU_DIM == 0 else f


def _dense_ffn_kernel(h1_ref, mb_ref, mc_ref, w1_ref, w3_ref, w2_ref, g_ref, b_ref, o_ref,
                      *, tf, n_ctx, alpha, fc):
    row = pl.program_id(1) * tf + lax.broadcasted_iota(jnp.int32, (tf, 1), 0)
    is_ctx = row < n_ctx
    sh2 = _row_mods(mb_ref, mc_ref, 3, is_ctx)
    sc2 = _row_mods(mb_ref, mc_ref, 4, is_ctx)
    g2 = _row_mods(mb_ref, mc_ref, 5, is_ctx)
    h1 = h1_ref[0]
    u = (h1 * (1.0 + sc2) + sh2).astype(BF16)
    mixed = _swiglu(u, w1_ref, w3_ref, w2_ref, fc)
    o_ref[0] = _layer_norm(alpha * h1 + (1.0 + g2) * mixed, g_ref[...], b_ref[...])


def _dense_ffn(h1, mod, w1, w3, w2, ln_g, ln_b, n_ctx, tf, alpha):
    b, t, d = h1.shape
    f = w1.shape[2]
    row_spec = pl.BlockSpec((1, tf, d), lambda i, j: (i, j, 0))
    vec = pl.BlockSpec((1, d), lambda i, j: (0, 0))
    const3 = lambda a: pl.BlockSpec(a.shape, lambda i, j: (0, 0, 0))
    return pl.pallas_call(
        functools.partial(_dense_ffn_kernel, tf=tf, n_ctx=n_ctx, alpha=alpha, fc=_ffn_chunk(f)),
        grid=(b, t // tf),
        in_specs=[
            row_spec,
            pl.BlockSpec((1, 6, d), lambda i, j: (i, 0, 0)),
            pl.BlockSpec((1, 6, d), lambda i, j: (b, 0, 0)),
            const3(w1), const3(w3), const3(w2), vec, vec,
        ],
        out_specs=row_spec,
        out_shape=jax.ShapeDtypeStruct((b, t, d), F32),
        compiler_params=_cparams(("arbitrary", "arbitrary")),
        name="dense_ffn",
    )(h1, mod, mod, w1, w3, w2, ln_g, ln_b)


def _moe_ffn_kernel(te_ref, nu_ref, src_ref, nxt_ref, u_ref, w1_ref, w3_ref, w2_ref, o_ref,
                    xbuf, sem, *, tm, fc):
    del te_ref
    i = pl.program_id(0)
    n_used = nu_ref[0]
    slot = i % 2

    def gather(idx_ref, dst_slot):
        for r in range(tm):
            pltpu.make_async_copy(u_ref.at[pl.ds(idx_ref[0, 0, r], 1)],
                                  xbuf.at[dst_slot, pl.ds(r, 1)], sem.at[dst_slot]).start()

    @pl.when(jnp.logical_and(i == 0, n_used > 0))
    def _():
        gather(src_ref, 0)

    @pl.when(i + 1 < n_used)
    def _():
        gather(nxt_ref, 1 - slot)

    @pl.when(i < n_used)
    def _():
        pltpu.make_async_copy(u_ref.at[pl.ds(0, tm)], xbuf.at[slot], sem.at[slot]).wait()
        o_ref[...] = _swiglu(xbuf[slot].astype(BF16), w1_ref, w3_ref, w2_ref, fc)

    @pl.when(i >= n_used)
    def _():
        o_ref[...] = jnp.zeros_like(o_ref)


def _moe_ffn(u, src, w1, w3, w2, layer, tile_expert, n_used, tm):
    n, d = u.shape
    f = w1.shape[3]
    n_tiles = src.shape[0] // tm
    src3 = src.reshape(n_tiles, 1, tm)
    wspec = lambda shape: pl.BlockSpec((None,) + shape, lambda i, te, nu: (layer, te[i], 0, 0))
    grid_spec = pltpu.PrefetchScalarGridSpec(
        num_scalar_prefetch=2,
        grid=(n_tiles,),
        in_specs=[
            pl.BlockSpec((1, 1, tm), lambda i, te, nu: (i, 0, 0), memory_space=pltpu.SMEM),
            pl.BlockSpec((1, 1, tm), lambda i, te, nu: (jnp.minimum(i + 1, n_tiles - 1), 0, 0),
                         memory_space=pltpu.SMEM),
            pl.BlockSpec(memory_space=pl.ANY),
            wspec((1, d, f)), wspec((1, d, f)), wspec((1, f, d)),
        ],
        out_specs=pl.BlockSpec((tm, d), lambda i, te, nu: (i, 0)),
        scratch_shapes=[pltpu.VMEM((2, tm, d), F32), pltpu.SemaphoreType.DMA((2,))],
    )
    return pl.pallas_call(
        functools.partial(_moe_ffn_kernel, tm=tm, fc=_ffn_chunk(f)),
        grid_spec=grid_spec,
        out_shape=jax.ShapeDtypeStruct((n_tiles * tm, d), F32),
        compiler_params=_cparams(("arbitrary",)),
        name="moe_ffn",
    )(tile_expert, n_used, src3, src3, u, w1, w3, w2)


def _combine_kernel(dst_ref, nxt_ref, ys_ref, route_ref, h1_ref, mb_ref, mc_ref, g_ref, b_ref,
                    o_ref, buf, sem, *, tc, n_ctx, first_tile, alpha):
    i, j = pl.program_id(0), pl.program_id(1)
    nj = pl.num_programs(1)
    s = i * nj + j
    total = pl.num_programs(0) * nj
    slot = s % 2

    def gather(idx_ref, dslot):
        for r in range(tc):
            for k in range(N_TOP):
                pltpu.make_async_copy(ys_ref.at[pl.ds(idx_ref[0, 0, N_TOP * r + k], 1)],
                                      buf.at[dslot, pl.ds(k * tc + r, 1)], sem.at[dslot]).start()

    @pl.when(s == 0)
    def _():
        gather(dst_ref, 0)

    @pl.when(s + 1 < total)
    def _():
        gather(nxt_ref, 1 - slot)

    pltpu.make_async_copy(ys_ref.at[pl.ds(0, N_TOP * tc)], buf.at[slot], sem.at[slot]).wait()
    rt = route_ref[...]
    mixed = rt[:, 2:3] * buf[slot, 0:tc] + rt[:, 3:4] * buf[slot, tc:N_TOP * tc]
    row = (first_tile + j) * tc + lax.broadcasted_iota(jnp.int32, (tc, 1), 0)
    g2 = _row_mods(mb_ref, mc_ref, 5, row < n_ctx)
    o_ref[0] = _layer_norm(alpha * h1_ref[0] + (1.0 + g2) * mixed, g_ref[...], b_ref[...])


def _moe_combine(h1, mod, ln_g, ln_b, ys, dst, route, n_ctx, tc, alpha, skip_ctx):
    b, t, d = h1.shape
    tiles = t // tc
    off = n_ctx // tc if skip_ctx else 0
    nj = tiles - off
    n_exp = route.shape[1]
    dst3 = dst.reshape(b * tiles, 1, N_TOP * tc)

    def cur(i, j):
        return i * tiles + off + j

    def nxt(i, j):
        i2 = jnp.minimum(i + (j + 1) // nj, b - 1)
        return i2 * tiles + off + (j + 1) % nj

    vec = pl.BlockSpec((1, d), lambda i, j: (0, 0))
    return pl.pallas_call(
        functools.partial(_combine_kernel, tc=tc, n_ctx=n_ctx, first_tile=off, alpha=alpha),
        grid=(b, nj),
        in_specs=[
            pl.BlockSpec((1, 1, N_TOP * tc), lambda i, j: (cur(i, j), 0, 0),
                         memory_space=pltpu.SMEM),
            pl.BlockSpec((1, 1, N_TOP * tc), lambda i, j: (nxt(i, j), 0, 0),
                         memory_space=pltpu.SMEM),
            pl.BlockSpec(memory_space=pl.ANY),
            pl.BlockSpec((tc, n_exp), lambda i, j: (cur(i, j), 0)),
            pl.BlockSpec((1, tc, d), lambda i, j: (i, off + j, 0)),
            pl.BlockSpec((1, 6, d), lambda i, j: (i, 0, 0)),
            pl.BlockSpec((1, 6, d), lambda i, j: (b, 0, 0)),
            vec, vec,
        ],
        out_specs=pl.BlockSpec((1, tc, d), lambda i, j: (i, j, 0)),
        out_shape=jax.ShapeDtypeStruct((b, nj * tc, d), F32),
        scratch_shapes=[pltpu.VMEM((2, N_TOP * tc, d), F32), pltpu.SemaphoreType.DMA((2,))],
        compiler_params=_cparams(("arbitrary", "arbitrary")),
        name="moe_combine",
    )(dst3, dst3, ys, route, h1, mod, mod, ln_g, ln_b)


def _rope_tables(n_ctx, seq):
    rows = seq // GRID_W
    pairs = HEAD_DIM // 4
    rpos = jnp.repeat(jnp.arange(rows, dtype=F32), GRID_W)
    cpos = jnp.tile(jnp.arange(GRID_W, dtype=F32), rows)
    inv_freq = ROPE_BASE ** (-jnp.arange(pairs, dtype=F32) / pairs)
    ang = jnp.concatenate([rpos[:, None] * inv_freq, cpos[:, None] * inv_freq], -1)
    cos, sin = jnp.cos(ang), jnp.sin(ang)
    reps = LANES // HEAD_DIM
    cs = jnp.tile(jnp.concatenate([cos, cos], -1), (1, reps))
    sn = jnp.tile(jnp.concatenate([-sin, sin], -1), (1, reps))
    cs = jnp.concatenate([jnp.ones((n_ctx, LANES), F32), cs], 0)
    sn = jnp.concatenate([jnp.zeros((n_ctx, LANES), F32), sn], 0)
    return cs, sn


def _prep_w_in(w_in):
    lead = w_in.shape[:-1]
    q0 = 3 * ATTN_W

    def split_pairs(w):
        w = w.reshape(*lead, ATTN_W // HEAD_DIM, HEAD_DIM // 2, 2)
        return jnp.swapaxes(w, -1, -2).reshape(*lead, ATTN_W)

    q_scale = HEAD_DIM ** -0.5 * math.log2(math.e)
    return jnp.concatenate([
        split_pairs(w_in[..., :ATTN_W]), w_in[..., ATTN_W:q0],
        split_pairs(w_in[..., q0:q0 + ATTN_W]) * q_scale, w_in[..., q0 + ATTN_W:]],
        axis=-1).astype(BF16)


def _prep_rg_gates(wa, wx):
    nd, g, bw, _ = wa.shape
    per = MXU_DIM // bw
    halves = g // per

    def bdiag(w):
        w = w.reshape(nd, halves, per, bw, bw)
        eye = jnp.eye(per, dtype=w.dtype)
        full = w[:, :, :, :, None, :] * eye[None, None, :, None, :, None]
        return full.reshape(nd, halves, per * bw, per * bw)

    return (0.5 * jnp.concatenate([bdiag(wa), bdiag(wx)], -1)).astype(BF16)


def _routing_tables(route, keep, n_exp, tm):
    e = route[:, :N_TOP].astype(jnp.int32).reshape(-1)
    keep2 = jnp.repeat(keep, N_TOP)
    oh = ((e[:, None] == jnp.arange(n_exp)[None, :]) & keep2[:, None]).astype(jnp.int32)
    csum = jnp.cumsum(oh, axis=0)
    rank = jnp.sum((csum - oh) * oh, axis=1)
    counts = csum[-1]
    tiles_per = (counts + tm - 1) // tm
    tile_end = jnp.cumsum(tiles_per)
    off = (tile_end - tiles_per) * tm
    n_tiles = e.shape[0] // tm + n_exp
    rows = n_tiles * tm
    dst = jnp.where(keep2, jnp.sum(oh * off[None, :], axis=1) + rank, rows).astype(jnp.int32)
    tile_ids = jnp.arange(n_tiles)
    tile_expert = jnp.sum((tile_ids[:, None] >= tile_end[None, :]).astype(jnp.int32), axis=1)
    tile_expert = jnp.minimum(tile_expert, n_exp - 1).astype(jnp.int32)
    bits = (e.shape[0] - 1).bit_length()
    keys = (jnp.where(keep2, e, n_exp) << bits) | jnp.arange(e.shape[0], dtype=jnp.int32)
    order = jnp.sort(keys) & ((1 << bits) - 1)
    row_expert = jnp.repeat(tile_expert, tm)
    k_in = jnp.arange(rows, dtype=jnp.int32) - off[row_expert]
    first = jnp.cumsum(counts) - counts
    slot = jnp.clip(first[row_expert] + k_in, 0, e.shape[0] - 1)
    src = jnp.where(k_in < counts[row_expert], order[slot] // N_TOP, 0).astype(jnp.int32)
    return dst, src, tile_expert, tile_end[-1:].astype(jnp.int32)


def kernel(x, c, ctx, c_ctx, w_mod, b_mod, w_in, lam_q1, lam_k1, lam_q2, lam_k2, subln_g,
           conv_w, conv_b, rg_wa, rg_ba, rg_wx, rg_bx, rg_lambda, w_out,
           ln1_g, ln1_b, ln2_g, ln2_b, ffn_w1, ffn_w3, ffn_w2,
           moe_router, moe_w1, moe_w3, moe_w2):
    b, seq, d = x.shape
    n_ctx = ctx.shape[1]
    t = n_ctx + seq
    depth = w_in.shape[0]
    alpha = (2 * depth) ** 0.25
    assert seq % GRID_W == 0 and d - ATTN_W == ATTN_W and t % n_ctx == 0

    tm = _pick_tile(t, 768)
    tf = _pick_tile(t, 768)
    te = 512 if (N_TOP * b * t) % 512 == 0 else _pick_tile(N_TOP * b * t, 512)
    tc = _pick_tile(n_ctx, 256)
    chunk = math.gcd(n_ctx, 256)

    h = jnp.concatenate([ctx, x], axis=1)
    mrows = -(-(b + 1) // SUBLANES) * SUBLANES
    cvec = jnp.zeros((mrows, d), F32).at[:b].set(c).at[b].set(c_ctx)
    mod_all = _modulation(cvec, w_mod, b_mod).reshape(depth, mrows, 6, d)
    cs_tab, sn_tab = _rope_tables(n_ctx, seq)

    w_in_p = _prep_w_in(w_in)
    w_out_b = w_out.astype(BF16)
    moe_w1b, moe_w3b, moe_w2b = (w.astype(BF16) for w in (moe_w1, moe_w3, moe_w2))
    lamv =jnp.stack([lam_q1, lam_k1, lam_q2, lam_k2], axis=1)
    rg_bias = jnp.stack([rg_ba, rg_bx], axis=2).reshape(depth, 4, -1)
    is_lat = jnp.tile(jnp.arange(t) >= n_ctx, b)

    for i in range(depth):
        last = i == depth - 1
        lam_init = 0.8 - 0.6 * math.exp(-0.3 * i)
        mod = mod_all[i]
        g2_ln = (ln2_g[i][None, :], ln2_b[i][None, :])
        k, v, xr, q, y = _in_proj(h, mod, w_in_p[i], cs_tab, sn_tab, n_ctx, tm)
        att = _attention(q, k, v, lamv[i], subln_g[i][None, :], n_ctx, lam_init)
        rg = _rglru(xr, y, conv_w[i], conv_b[i][None, :], _prep_rg_gates(rg_wa[i], rg_wx[i]),
                    rg_bias[i], rg_lambda[i], n_ctx, chunk)
        j = i // 2
        router = moe_router[j] if i % 2 else None
        outs = _out_proj(att, rg, h, mod, w_out_b[i], ln1_g[i][None, :], ln1_b[i][None, :],
                         router, n_ctx, tm, alpha)
        h1 = outs[0]
        if i % 2 == 0:
            h = _dense_ffn(h1, mod, ffn_w1[j][None].astype(BF16), ffn_w3[j][None].astype(BF16),
                           ffn_w2[j][None].astype(BF16), *g2_ln, n_ctx, tf, alpha)
        else:
            u, route = outs[1], outs[2]
            n_exp = route.shape[1]
            keep = is_lat if last else jnp.ones((b * t,), bool)
            dst, src, tile_expert, n_used = _routing_tables(route, keep, n_exp, te)
            ys = _moe_ffn(u.reshape(b * t, d), src, moe_w1b, moe_w3b, moe_w2b, j,
                          tile_expert, n_used, te)
            h = _moe_combine(h1, mod, *g2_ln, ys, dst, route, n_ctx, tc, alpha, skip_ctx=last)
    return h if h.shape[1] == seq else h[:, n_ctx:, :]
```

```python
import functools
import math

import jax
import jax.numpy as jnp
from jax import lax
from jax.experimental import pallas as pl
from jax.experimental.pallas import tpu as pltpu

F32 = jnp.float32
BF16 = jnp.bfloat16

HEAD_DIM = 64
ATTN_HEADS = 4
HEAD_W = 2 * HEAD_DIM
ATTN_W = ATTN_HEADS * HEAD_W
GRID_W = 64
RG_BLOCKS = 8
RG_C = 8.0
CONV_W = 4
N_TOP = 2
ROPE_BASE = 10000.0
LN_EPS = 1e-5
RMS_EPS = 1e-5

LANES = 128
SUBLANES = 8
MXU_DIM = 256
VMEM_LIMIT = 56 * 1024 * 1024


def _cparams(sem, vmem=VMEM_LIMIT):
    return pltpu.CompilerParams(dimension_semantics=sem, vmem_limit_bytes=vmem)


def _pick_tile(n, pref):
    t = min(n, pref)
    while t > 8 and (n % t or t % 8):
        t -= 8
    assert n % t == 0
    return t


def _layer_norm(z, g, b):
    mu = jnp.mean(z, -1, keepdims=True)
    zc = z - mu
    var = jnp.mean(zc * zc, -1, keepdims=True)
    return zc * lax.rsqrt(var + LN_EPS) * g + b


def _row_mods(mb_ref, mc_ref, idx, is_ctx):
    return jnp.where(is_ctx, mc_ref[0, idx:idx + 1, :], mb_ref[0, idx:idx + 1, :])


def _sigmoid(x):
    return 0.5 * jnp.tanh(0.5 * x) + 0.5


def _mod_kernel(c_ref, w_ref, b_ref, o_ref):
    cv = c_ref[...]
    s = (cv * _sigmoid(cv)).astype(BF16)
    o_ref[0] = jnp.dot(s, w_ref[0].astype(BF16), preferred_element_type=F32) + b_ref[0]


def _modulation(cvec, w_mod, b_mod):
    depth, d, n = w_mod.shape
    mb = cvec.shape[0]
    tn = _pick_tile(n, 1536)
    return pl.pallas_call(
        _mod_kernel,
        grid=(depth, n // tn),
        in_specs=[
            pl.BlockSpec((mb, d), lambda i, j: (0, 0)),
            pl.BlockSpec((1, d, tn), lambda i, j: (i, 0, j)),
            pl.BlockSpec((1, 1, tn), lambda i, j: (i, 0, j)),
        ],
        out_specs=pl.BlockSpec((1, mb, tn), lambda i, j: (i, 0, j)),
        out_shape=jax.ShapeDtypeStruct((depth, mb, n), F32),
        compiler_params=_cparams(("arbitrary", "arbitrary")),
        name="adaln_mod",
    )(cvec, w_mod, b_mod.reshape(depth, 1, n))


def _rope_store(t, cs, sn, out_ref):
    lane = lax.broadcasted_iota(jnp.int32, (1, LANES), 1)
    first = (lane % HEAD_DIM) < (HEAD_DIM // 2)
    for g in range(ATTN_W // LANES):
        tg = t[:, g * LANES:(g + 1) * LANES]
        sw = jnp.where(first, pltpu.roll(tg, LANES - HEAD_DIM // 2, 1),
                       pltpu.roll(tg, HEAD_DIM // 2, 1))
        out_ref[0, :, g * LANES:(g + 1) * LANES] = (tg * cs + sw * sn).astype(out_ref.dtype)


def _in_kernel(h_ref, mb_ref, mc_ref, w_ref, cs_ref, sn_ref,
               k_ref, v_ref, xr_ref, q_ref, y_ref, *, tm, n_ctx):
    j = pl.program_id(1)
    row = j * tm + lax.broadcasted_iota(jnp.int32, (tm, 1), 0)
    is_ctx = row < n_ctx
    shift = _row_mods(mb_ref, mc_ref, 0, is_ctx)
    scale = _row_mods(mb_ref, mc_ref, 1, is_ctx)
    x = (h_ref[0] * (1.0 + scale) + shift).astype(BF16)
    cs = cs_ref[...]
    sn = sn_ref[...]
    w = ATTN_W

    def proj(part):
        return jnp.dot(x, w_ref[:, part * w:(part + 1) * w], preferred_element_type=F32)

    _rope_store(proj(0), cs, sn, k_ref)
    v_ref[0] = proj(1).astype(v_ref.dtype)
    xr_ref[0] = proj(2)
    _rope_store(proj(3), cs, sn, q_ref)
    y_ref[0] = proj(4)


def _in_proj(h, mod, w_in, cs_tab, sn_tab, n_ctx, tm):
    b, t, d = h.shape
    n = w_in.shape[1]
    w = ATTN_W
    assert n == 5 * w
    row_spec = lambda width: pl.BlockSpec((1, tm, width), lambda i, j: (i, j, 0))
    return pl.pallas_call(
        functools.partial(_in_kernel, tm=tm, n_ctx=n_ctx),
        grid=(b, t // tm),
        in_specs=[
            row_spec(d),
            pl.BlockSpec((1, 6, d), lambda i, j: (i, 0, 0)),
            pl.BlockSpec((1, 6, d), lambda i, j: (b, 0, 0)),
            pl.BlockSpec((d, n), lambda i, j: (0, 0)),
            pl.BlockSpec((tm, LANES), lambda i, j: (j, 0)),
            pl.BlockSpec((tm, LANES), lambda i, j: (j, 0)),
        ],
        out_specs=[row_spec(w)] * 5,
        out_shape=[
            jax.ShapeDtypeStruct((b, t, w), BF16),
            jax.ShapeDtypeStruct((b, t, w), BF16),
            jax.ShapeDtypeStruct((b, t, w), F32),
            jax.ShapeDtypeStruct((b, t, w), BF16),
            jax.ShapeDtypeStruct((b, t, w), F32),
        ],
        compiler_params=_cparams(("arbitrary", "arbitrary")),
        name="in_proj",
    )(h, mod, mod, w_in, cs_tab, sn_tab)


def _attn_kernel(q_ref, k_ref, v_ref, lamv_ref, g_ref, o_ref, vext_sc, *, n_ctx, n_all, lam_init):
    j = pl.program_id(1)
    lv = lamv_ref[...]
    lam = (jnp.exp(jnp.sum(lv[0:1] * lv[1:2], keepdims=True))
           - jnp.exp(jnp.sum(lv[2:3] * lv[3:4], keepdims=True)) + lam_init)
    gain = g_ref[...] * (1.0 - lam_init)
    lane = lax.broadcasted_iota(jnp.int32, (1, HEAD_W), 1)
    dn = (((1,), (1,)), ((), ()))

    def softmax_av(qm, kh, vext):
        s = lax.dot_general(qm, kh, dn, preferred_element_type=F32)
        p = jnp.exp2(s - jnp.max(s, -1, keepdims=True)).astype(BF16)
        oe = jnp.dot(p, vext, preferred_element_type=F32)
        return oe[:, :HEAD_W] / oe[:, HEAD_W:HEAD_W + 1]

    @pl.when(j == 0)
    def _():
        for h in range(ATTN_HEADS):
            vext_sc[h, :, 0:HEAD_W] = v_ref[0, :, h * HEAD_W:(h + 1) * HEAD_W]
            vext_sc[h, :, HEAD_W:] = jnp.ones((n_all, HEAD_W), BF16)

    def run(nk):
        for h in range(ATTN_HEADS):
            sl = slice(h * HEAD_W, (h + 1) * HEAD_W)
            qh = q_ref[0, :, sl]
            kh = k_ref[0, 0:nk, sl]
            vext = vext_sc[h, 0:nk, :]
            q1 = jnp.where(lane < HEAD_DIM, qh, jnp.zeros_like(qh))
            q2 = jnp.where(lane >= HEAD_DIM, qh, jnp.zeros_like(qh))
            o = softmax_av(q1, kh, vext) - lam * softmax_av(q2, kh, vext)
            o = o * lax.rsqrt(jnp.mean(o * o, -1, keepdims=True) + RMS_EPS) * gain
            o_ref[0, :, sl] = o.astype(o_ref.dtype)

    def run_small(nk):
        sls = [slice(h * HEAD_W, (h + 1) * HEAD_W) for h in range(ATTN_HEADS)]
        masks = (lane < HEAD_DIM, lane >= HEAD_DIM)
        ss = []
        for sl in sls:
            qh = q_ref[0, :, sl]
            kh = k_ref[0, 0:nk, sl]
            for mk in masks:
                ss.append(lax.dot_general(jnp.where(mk, qh, jnp.zeros_like(qh)), kh, dn,
                                          preferred_element_type=F32))
        mx = [jnp.max(s, -1, keepdims=True) for s in ss]
        ps = [jnp.exp2(s - m).astype(BF16) for s, m in zip(ss, mx)]
        vexts = [vext_sc[h, 0:nk, :] for h in range(ATTN_HEADS)]
        oes = [jnp.dot(p, vexts[n // 2], preferred_element_type=F32) for n, p in enumerate(ps)]
        ons = [oe[:, :HEAD_W] / oe[:, HEAD_W:HEAD_W + 1] for oe in oes]
        os_ = [ons[2 * h] - lam * ons[2 * h + 1] for h in range(ATTN_HEADS)]
        ms = [jnp.mean(o * o, -1, keepdims=True) for o in os_]
        for h, sl in enumerate(sls):
            o_ref[0, :, sl] = (os_[h] * lax.rsqrt(ms[h] + RMS_EPS) * gain).astype(o_ref.dtype)

    @pl.when(j == 0)
    def _():
        run_small(n_ctx)

    @pl.when(j > 0)
    def _():
        run(n_all)


def _attention(q, k, v, lamv, gain, n_ctx, lam_init):
    b, t, w = q.shape
    tq = n_ctx
    return pl.pallas_call(
        functools.partial(_attn_kernel, n_ctx=n_ctx, n_all=t, lam_init=lam_init),
        grid=(b, t // tq),
        in_specs=[
            pl.BlockSpec((1, tq, w), lambda i, j: (i, j, 0)),
            pl.BlockSpec((1, t, w), lambda i, j: (i, 0, 0)),
            pl.BlockSpec((1, t, w), lambda i, j: (i, 0, 0)),
            pl.BlockSpec((4, HEAD_DIM), lambda i, j: (0, 0)),
            pl.BlockSpec((1, HEAD_W), lambda i, j: (0, 0)),
        ],
        out_specs=pl.BlockSpec((1, tq, w), lambda i, j: (i, j, 0)),
        out_shape=jax.ShapeDtypeStruct((b, t, w), BF16),
        scratch_shapes=[pltpu.VMEM((ATTN_HEADS, t, 2 * HEAD_W), BF16)],
        compiler_params=_cparams(("arbitrary", "arbitrary")),
        name="diff_attn",
    )(q, k, v, lamv, gain)


def _gelu_tanh(x):
    return 0.5 * x * (1.0 + jnp.tanh(math.sqrt(2.0 / math.pi) * (x + 0.044715 * x * x * x)))


def _softplus(z):
    return jnp.maximum(z, 0.0) + jnp.log1p(jnp.exp(-jnp.abs(z)))


def _rg_kernel(xr_ref, y_ref, cw_ref, cb_ref, wg_ref, bias_ref, lam_ref, o_ref,
               xc_sc, acc_sc, a_sl, b_sl, *, n_ctx, n_all, chunk):
    t, c, r = n_all, n_ctx, chunk
    half = MXU_DIM
    rgw = xc_sc.shape[1]
    groups = rgw // LANES
    nseg = SUBLANES
    seg = r // nseg

    row = lax.broadcasted_iota(jnp.int32, (t, 1), 0)
    in_ctx = row < c
    tl = jnp.where(in_ctx, row, row - c)
    span = jnp.where(in_ctx, c, t - c)
    for g in range(groups):
        sl = slice(g * LANES, (g + 1) * LANES)
        xg = xr_ref[0, :, sl]
        wv = cw_ref[:, sl]
        acc = cb_ref[:, sl] + xg * wv[2:3]
        acc = acc + jnp.where(tl >= 2, pltpu.roll(xg, 2, 0), 0.0) * wv[0:1]
        acc = acc + jnp.where(tl >= 1, pltpu.roll(xg, 1, 0), 0.0) * wv[1:2]
        acc = acc + jnp.where(tl + 1 < span, pltpu.roll(xg, t - 1, 0), 0.0) * wv[3:4]
        xc_sc[:, sl] = acc

    n_chunks = t // r
    ctx_chunks = c // r
    for d in range(2):
        c4 = (-0.5 * RG_C) * _softplus(-lam_ref[d:d + 1, :])
        ba = 0.5 * bias_ref[2 * d:2 * d + 1, :]
        bx = 0.5 * bias_ref[2 * d + 1:2 * d + 2, :]

        def chunk_step(s, hc, d=d, c4=c4, ba=ba, bx=bx):
            if d == 0:
                ci = s
            else:
                ci = jnp.where(s < ctx_chunks, ctx_chunks - 1 - s,
                               n_chunks - 1 - (s - ctx_chunks))
            r0 = pl.multiple_of(ci * r, r)
            xc = xc_sc[pl.ds(r0, r), :]
            xb = xc.astype(BF16)
            for hh in range(rgw // half):
                hs = slice(hh * half, (hh + 1) * half)
                g2 = jnp.dot(xb[:, hs], wg_ref[d, hh], preferred_element_type=F32)
                tr = jnp.tanh(g2[:, :half] + ba[:, hs])
                gi = 0.5 * jnp.tanh(g2[:, half:] + bx[:, hs]) + 0.5
                log_a = c4[:, hs] * (tr + 1.0)
                a = jnp.exp(log_a)
                th = jnp.tanh(log_a)
                bt = jnp.sqrt(-2.0 * th) * lax.rsqrt(1.0 - th) * (gi * xc[:, hs])
                for gg in range(half // LANES):
                    g = hh * (half // LANES) + gg
                    ls = slice(gg * LANES, (gg + 1) * LANES)
                    for k in range(nseg):
                        rows = slice(k * seg, (k + 1) * seg)
                        a_sl[g, pl.ds(k, seg, stride=nseg), :] = a[rows, ls]
                        b_sl[g, pl.ds(k, seg, stride=nseg), :] = bt[rows, ls]

            def seg_step(ii, carry):
                i = ii if d == 0 else seg - 1 - ii
                base = pl.multiple_of(i * nseg, nseg)
                hs_, as_ = carry
                nh, na = [], []
                for g in range(groups):
                    av = a_sl[g, pl.ds(base, nseg), :]
                    hv = av * hs_[g] + b_sl[g, pl.ds(base, nseg), :]
                    pv = av * as_[g]
                    b_sl[g, pl.ds(base, nseg), :] = hv
                    a_sl[g, pl.ds(base, nseg), :] = pv
                    nh.append(hv)
                    na.append(pv)
                return tuple(nh), tuple(na)

            zero = jnp.zeros((nseg, LANES), F32)
            one = jnp.ones((nseg, LANES), F32)
            h_end, a_end = lax.fori_loop(
                0, seg, seg_step, ((zero,) * groups, (one,) * groups), unroll=4)

            order = range(nseg) if d == 0 else range(nseg - 1, -1, -1)
            new_carry = []
            for g in range(groups):
                ls = slice(g * LANES, (g + 1) * LANES)
                h_in = hc[:, ls]
                for k in order:
                    rows = pl.ds(r0 + k * seg, seg)
                    h_true = (b_sl[g, pl.ds(k, seg, stride=nseg), :]
                              + a_sl[g, pl.ds(k, seg, stride=nseg), :] * h_in)
                    if d == 0:
                        acc_sc[rows, ls] = h_true
                    else:
                        tot = acc_sc[rows, ls] + h_true
                        o_ref[0, rows, ls] = (tot * _gelu_tanh(y_ref[0, rows, ls])).astype(o_ref.dtype)
                    h_in = h_end[g][k:k + 1, :] + a_end[g][k:k + 1, :] * h_in
                new_carry.append(h_in)
            return jnp.concatenate(new_carry, axis=1)

        lax.fori_loop(0, n_chunks, chunk_step, jnp.zeros((1, rgw), F32))


def _rglru(xr, y, conv_w, conv_b, wg, bias, lam, n_ctx, chunk):
    b, t, w = xr.shape
    full = lambda a: pl.BlockSpec(a.shape, lambda i: (0,) * a.ndim)
    seq = pl.BlockSpec((1, t, w), lambda i: (i, 0, 0))
    return pl.pallas_call(
        functools.partial(_rg_kernel, n_ctx=n_ctx, n_all=t, chunk=chunk),
        grid=(b,),
        in_specs=[seq, seq, full(conv_w), full(conv_b), full(wg), full(bias), full(lam)],
        out_specs=seq,
        out_shape=jax.ShapeDtypeStruct((b, t, w), BF16),
        scratch_shapes=[
            pltpu.VMEM((t, w), F32),
            pltpu.VMEM((t, w), F32),
            pltpu.VMEM((w // LANES, chunk, LANES), F32),
            pltpu.VMEM((w // LANES, chunk, LANES), F32),
        ],
        compiler_params=_cparams(("arbitrary",)),
        name="rglru",
    )(xr, y, conv_w, conv_b, wg, bias, lam)


def _out_kernel(*refs, tm, n_ctx, alpha, moe):
    if moe:
        (att_ref, rg_ref, h_ref, mb_ref, mc_ref, w_ref, g_ref, b_ref, rt_ref,
         h1_ref, u_ref, route_ref) = refs
    else:
        att_ref, rg_ref, h_ref, mb_ref, mc_ref, w_ref, g_ref, b_ref, h1_ref = refs
    j = pl.program_id(1)
    row = j * tm + lax.broadcasted_iota(jnp.int32, (tm, 1), 0)
    is_ctx = row < n_ctx
    aw = att_ref.shape[2]
    mix = (jnp.dot(att_ref[0], w_ref[0:aw, :], preferred_element_type=F32)
           + jnp.dot(rg_ref[0], w_ref[aw:, :], preferred_element_type=F32))
    g1 = _row_mods(mb_ref, mc_ref, 2, is_ctx)
    h1 = _layer_norm(alpha * h_ref[0] + (1.0 + g1) * mix, g_ref[...], b_ref[...])
    h1_ref[0] = h1
    if not moe:
        return
    sh2 = _row_mods(mb_ref, mc_ref, 3, is_ctx)
    sc2 = _row_mods(mb_ref, mc_ref, 4, is_ctx)
    u = h1 * (1.0 + sc2) + sh2
    u_ref[0] = u
    n_exp = route_ref.shape[1]
    u_hi = u.astype(BF16)
    u_lo = (u - u_hi.astype(F32)).astype(BF16)
    prod = (jnp.dot(u_hi, rt_ref[...], preferred_element_type=F32)
            + jnp.dot(u_lo, rt_ref[...], preferred_element_type=F32))
    logits = prod[:, :LANES] + prod[:, LANES:]
    lane = lax.broadcasted_iota(jnp.int32, (1, LANES), 1)
    neg = -jnp.inf
    lg = jnp.where(lane < n_exp, logits, neg)
    m1 = jnp.max(lg, -1, keepdims=True)
    i1 = jnp.min(jnp.where(lg == m1, lane, LANES), -1, keepdims=True)
    lg2 = jnp.where(lane == i1, neg, lg)
    m2 = jnp.max(lg2, -1, keepdims=True)
    i2 = jnp.min(jnp.where(lg2 == m2, lane, LANES), -1, keepdims=True)
    e2 = jnp.exp(m2 - m1)
    den = 1.0 + e2
    route = jnp.where(lane == 0, i1.astype(F32),
                      jnp.where(lane == 1, i2.astype(F32),
                                jnp.where(lane == 2, 1.0 / den,
                                          jnp.where(lane == 3, e2 / den, 0.0))))
    route_ref[...] = route[:, :n_exp]


def _out_proj(att, rg, h, mod, w_out, ln_g, ln_b, router, n_ctx, tm, alpha):
    b, t, d = h.shape
    aw = att.shape[2]
    moe = router is not None
    row_spec = lambda width: pl.BlockSpec((1, tm, width), lambda i, j: (i, j, 0))
    vec = pl.BlockSpec((1, d), lambda i, j: (0, 0))
    in_specs = [
        row_spec(aw), row_spec(aw), row_spec(d),
        pl.BlockSpec((1, 6, d), lambda i, j: (i, 0, 0)),
        pl.BlockSpec((1, 6, d), lambda i, j: (b, 0, 0)),
        pl.BlockSpec((d, d), lambda i, j: (0, 0)),
        vec, vec,
    ]
    args = [att, rg, h, mod, mod, w_out, ln_g, ln_b]
    out_specs = [row_spec(d)]
    out_shape = [jax.ShapeDtypeStruct((b, t, d), F32)]
    if moe:
        n_exp = router.shape[1]
        r_hi = router.astype(BF16)
        r_lo = (router - r_hi.astype(F32)).astype(BF16)
        pad = lambda a: jnp.zeros((d, LANES), BF16).at[:, :n_exp].set(a)
        in_specs.append(pl.BlockSpec((d, 2 * LANES), lambda i, j: (0, 0)))
        args.append(jnp.concatenate([pad(r_hi), pad(r_lo)], axis=1))
        tiles = t // tm
        out_specs += [row_spec(d), pl.BlockSpec((tm, n_exp), lambda i, j: (i * tiles + j, 0))]
        out_shape += [jax.ShapeDtypeStruct((b, t, d), F32),
                      jax.ShapeDtypeStruct((b * t, n_exp), F32)]
    return pl.pallas_call(
        functools.partial(_out_kernel, tm=tm, n_ctx=n_ctx, alpha=alpha, moe=moe),
        grid=(b, t // tm),
        in_specs=in_specs,
        out_specs=out_specs,
        out_shape=out_shape,
        compiler_params=_cparams(("arbitrary", "arbitrary")),
        name="out_proj_moe" if moe else "out_proj",
    )(*args)


def _swiglu(x, w1_ref, w3_ref, w2_ref, fc):
    f = w1_ref.shape[2]
    acc = None
    for c in range(f // fc):
        cs = slice(c * fc, (c + 1) * fc)
        a = jnp.dot(x, w1_ref[0, :, cs], preferred_element_type=F32)
        bgate = jnp.dot(x, w3_ref[0, :, cs], preferred_element_type=F32)
        gact = (a * _sigmoid(a) * bgate).astype(BF16)
        part = jnp.dot(gact, w2_ref[0, cs, :], preferred_element_type=F32)
        acc = part if acc is None else acc + part
    return acc


def _ffn_chunk(f):
    return MXU_DIM if f % MXU_DIM == 0 else f


def _dense_ffn_kernel(h1_ref, mb_ref, mc_ref, w1_ref, w3_ref, w2_ref, g_ref, b_ref, o_ref,
                      *, tf, n_ctx, alpha, fc):
    row = pl.program_id(1) * tf + lax.broadcasted_iota(jnp.int32, (tf, 1), 0)
    is_ctx = row < n_ctx
    sh2 = _row_mods(mb_ref, mc_ref, 3, is_ctx)
    sc2 = _row_mods(mb_ref, mc_ref, 4, is_ctx)
    g2 = _row_mods(mb_ref, mc_ref, 5, is_ctx)
    h1 = h1_ref[0]
    u = (h1 * (1.0 + sc2) + sh2).astype(BF16)
    mixed = _swiglu(u, w1_ref, w3_ref, w2_ref, fc)
    o_ref[0] = _layer_norm(alpha * h1 + (1.0 + g2) * mixed, g_ref[...], b_ref[...])


def _dense_ffn(h1, mod, w1, w3, w2, ln_g, ln_b, n_ctx, tf, alpha):
    b, t, d = h1.shape
    f = w1.shape[2]
    row_spec = pl.BlockSpec((1, tf, d), lambda i, j: (i, j, 0))
    vec = pl.BlockSpec((1, d), lambda i, j: (0, 0))
    const3 = lambda a: pl.BlockSpec(a.shape, lambda i, j: (0, 0, 0))
    return pl.pallas_call(
        functools.partial(_dense_ffn_kernel, tf=tf, n_ctx=n_ctx, alpha=alpha, fc=_ffn_chunk(f)),
        grid=(b, t // tf),
        in_specs=[
            row_spec,
            pl.BlockSpec((1, 6, d), lambda i, j: (i, 0, 0)),
            pl.BlockSpec((1, 6, d), lambda i, j: (b, 0, 0)),
            const3(w1), const3(w3), const3(w2), vec, vec,
        ],
        out_specs=row_spec,
        out_shape=jax.ShapeDtypeStruct((b, t, d), F32),
        compiler_params=_cparams(("arbitrary", "arbitrary")),
        name="dense_ffn",
    )(h1, mod, mod, w1, w3, w2, ln_g, ln_b)


def _moe_ffn_kernel(te_ref, nu_ref, src_ref, nxt_ref, u_ref, w1_ref, w3_ref, w2_ref, o_ref,
                    xbuf, sem, *, tm, fc):
    del te_ref
    i = pl.program_id(0)
    n_used = nu_ref[0]
    slot = i % 2

    def gather(idx_ref, dst_slot):
        for r in range(tm):
            pltpu.make_async_copy(u_ref.at[pl.ds(idx_ref[0, 0, r], 1)],
                                  xbuf.at[dst_slot, pl.ds(r, 1)], sem.at[dst_slot]).start()

    @pl.when(jnp.logical_and(i == 0, n_used > 0))
    def _():
        gather(src_ref, 0)

    @pl.when(i + 1 < n_used)
    def _():
        gather(nxt_ref, 1 - slot)

    @pl.when(i < n_used)
    def _():
        pltpu.make_async_copy(u_ref.at[pl.ds(0, tm)], xbuf.at[slot], sem.at[slot]).wait()
        o_ref[...] = _swiglu(xbuf[slot].astype(BF16), w1_ref, w3_ref, w2_ref, fc)

    @pl.when(i >= n_used)
    def _():
        o_ref[...] = jnp.zeros_like(o_ref)


def _moe_ffn(u, src, w1, w3, w2, layer, tile_expert, n_used, tm):
    n, d = u.shape
    f = w1.shape[3]
    n_tiles = src.shape[0] // tm
    src3 = src.reshape(n_tiles, 1, tm)
    wspec = lambda shape: pl.BlockSpec((None,) + shape, lambda i, te, nu: (layer, te[i], 0, 0))
    grid_spec = pltpu.PrefetchScalarGridSpec(
        num_scalar_prefetch=2,
        grid=(n_tiles,),
        in_specs=[
            pl.BlockSpec((1, 1, tm), lambda i, te, nu: (i, 0, 0), memory_space=pltpu.SMEM),
            pl.BlockSpec((1, 1, tm), lambda i, te, nu: (jnp.minimum(i + 1, n_tiles - 1), 0, 0),
                         memory_space=pltpu.SMEM),
            pl.BlockSpec(memory_space=pl.ANY),
            wspec((1, d, f)), wspec((1, d, f)), wspec((1, f, d)),
        ],
        out_specs=pl.BlockSpec((tm, d), lambda i, te, nu: (i, 0)),
        scratch_shapes=[pltpu.VMEM((2, tm, d), F32), pltpu.SemaphoreType.DMA((2,))],
    )
    return pl.pallas_call(
        functools.partial(_moe_ffn_kernel, tm=tm, fc=_ffn_chunk(f)),
        grid_spec=grid_spec,
        out_shape=jax.ShapeDtypeStruct((n_tiles * tm, d), F32),
        compiler_params=_cparams(("arbitrary",)),
        name="moe_ffn",
    )(tile_expert, n_used, src3, src3, u, w1, w3, w2)


def _combine_kernel(dst_ref, nxt_ref, ys_ref, route_ref, h1_ref, mb_ref, mc_ref, g_ref, b_ref,
                    o_ref, buf, sem, *, tc, n_ctx, first_tile, alpha):
    i, j = pl.program_id(0), pl.program_id(1)
    nj = pl.num_programs(1)
    s = i * nj + j
    total = pl.num_programs(0) * nj
    slot = s % 2

    def gather(idx_ref, dslot):
        for r in range(tc):
            for k in range(N_TOP):
                pltpu.make_async_copy(ys_ref.at[pl.ds(idx_ref[0, 0, N_TOP * r + k], 1)],
                                      buf.at[dslot, pl.ds(k * tc + r, 1)], sem.at[dslot]).start()

    @pl.when(s == 0)
    def _():
        gather(dst_ref, 0)

    @pl.when(s + 1 < total)
    def _():
        gather(nxt_ref, 1 - slot)

    pltpu.make_async_copy(ys_ref.at[pl.ds(0, N_TOP * tc)], buf.at[slot], sem.at[slot]).wait()
    rt = route_ref[...]
    mixed = rt[:, 2:3] * buf[slot, 0:tc] + rt[:, 3:4] * buf[slot, tc:N_TOP * tc]
    row = (first_tile + j) * tc + lax.broadcasted_iota(jnp.int32, (tc, 1), 0)
    g2 = _row_mods(mb_ref, mc_ref, 5, row < n_ctx)
    o_ref[0] = _layer_norm(alpha * h1_ref[0] + (1.0 + g2) * mixed, g_ref[...], b_ref[...])


def _moe_combine(h1, mod, ln_g, ln_b, ys, dst, route, n_ctx, tc, alpha, skip_ctx):
    b, t, d = h1.shape
    tiles = t // tc
    off = n_ctx // tc if skip_ctx else 0
    nj = tiles - off
    n_exp = route.shape[1]
    dst3 = dst.reshape(b * tiles, 1, N_TOP * tc)

    def cur(i, j):
        return i * tiles + off + j

    def nxt(i, j):
        i2 = jnp.minimum(i + (j + 1) // nj, b - 1)
        return i2 * tiles + off + (j + 1) % nj

    vec = pl.BlockSpec((1, d), lambda i, j: (0, 0))
    return pl.pallas_call(
        functools.partial(_combine_kernel, tc=tc, n_ctx=n_ctx, first_tile=off, alpha=alpha),
        grid=(b, nj),
        in_specs=[
            pl.BlockSpec((1, 1, N_TOP * tc), lambda i, j: (cur(i, j), 0, 0),
                         memory_space=pltpu.SMEM),
            pl.BlockSpec((1, 1, N_TOP * tc), lambda i, j: (nxt(i, j), 0, 0),
                         memory_space=pltpu.SMEM),
            pl.BlockSpec(memory_space=pl.ANY),
            pl.BlockSpec((tc, n_exp), lambda i, j: (cur(i, j), 0)),
            pl.BlockSpec((1, tc, d), lambda i, j: (i, off + j, 0)),
            pl.BlockSpec((1, 6, d), lambda i, j: (i, 0, 0)),
            pl.BlockSpec((1, 6, d), lambda i, j: (b, 0, 0)),
            vec, vec,
        ],
        out_specs=pl.BlockSpec((1, tc, d), lambda i, j: (i, j, 0)),
        out_shape=jax.ShapeDtypeStruct((b, nj * tc, d), F32),
        scratch_shapes=[pltpu.VMEM((2, N_TOP * tc, d), F32), pltpu.SemaphoreType.DMA((2,))],
        compiler_params=_cparams(("arbitrary", "arbitrary")),
        name="moe_combine",
    )(dst3, dst3, ys, route, h1, mod, mod, ln_g, ln_b)


def _rope_tables(n_ctx, seq):
    rows = seq // GRID_W
    pairs = HEAD_DIM // 4
    rpos = jnp.repeat(jnp.arange(rows, dtype=F32), GRID_W)
    cpos = jnp.tile(jnp.arange(GRID_W, dtype=F32), rows)
    inv_freq = ROPE_BASE ** (-jnp.arange(pairs, dtype=F32) / pairs)
    ang = jnp.concatenate([rpos[:, None] * inv_freq, cpos[:, None] * inv_freq], -1)
    cos, sin = jnp.cos(ang), jnp.sin(ang)
    reps = LANES // HEAD_DIM
    cs = jnp.tile(jnp.concatenate([cos, cos], -1), (1, reps))
    sn = jnp.tile(jnp.concatenate([-sin, sin], -1), (1, reps))
    cs = jnp.concatenate([jnp.ones((n_ctx, LANES), F32), cs], 0)
    sn = jnp.concatenate([jnp.zeros((n_ctx, LANES), F32), sn], 0)
    return cs, sn


def _prep_w_in(w_in):
    lead = w_in.shape[:-1]
    q0 = 3 * ATTN_W

    def split_pairs(w):
        w = w.reshape(*lead, ATTN_W // HEAD_DIM, HEAD_DIM // 2, 2)
        return jnp.swapaxes(w, -1, -2).reshape(*lead, ATTN_W)

    q_scale = HEAD_DIM ** -0.5 * math.log2(math.e)
    return jnp.concatenate([
        split_pairs(w_in[..., :ATTN_W]), w_in[..., ATTN_W:q0],
        split_pairs(w_in[..., q0:q0 + ATTN_W]) * q_scale, w_in[..., q0 + ATTN_W:]],
        axis=-1).astype(BF16)


def _prep_rg_gates(wa, wx):
    nd, g, bw, _ = wa.shape
    per = MXU_DIM // bw
    halves = g // per

    def bdiag(w):
        w = w.reshape(nd, halves, per, bw, bw)
        eye = jnp.eye(per, dtype=w.dtype)
        full = w[:, :, :, :, None, :] * eye[None, None, :, None, :, None]
        return full.reshape(nd, halves, per * bw, per * bw)

    return (0.5 * jnp.concatenate([bdiag(wa), bdiag(wx)], -1)).astype(BF16)


def _routing_tables(route, keep, n_exp, tm):
    e = route[:, :N_TOP].astype(jnp.int32).reshape(-1)
    keep2 = jnp.repeat(keep, N_TOP)
    oh = ((e[:, None] == jnp.arange(n_exp)[None, :]) & keep2[:, None]).astype(jnp.int32)
    csum = jnp.cumsum(oh, axis=0)
    rank = jnp.sum((csum - oh) * oh, axis=1)
    counts = csum[-1]
    tiles_per = (counts + tm - 1) // tm
    tile_end = jnp.cumsum(tiles_per)
    off = (tile_end - tiles_per) * tm
    n_tiles = e.shape[0] // tm + n_exp
    rows = n_tiles * tm
    dst = jnp.where(keep2, jnp.sum(oh * off[None, :], axis=1) + rank, rows).astype(jnp.int32)
    tile_ids = jnp.arange(n_tiles)
    tile_expert = jnp.sum((tile_ids[:, None] >= tile_end[None, :]).astype(jnp.int32), axis=1)
    tile_expert = jnp.minimum(tile_expert, n_exp - 1).astype(jnp.int32)
    bits = (e.shape[0] - 1).bit_length()
    keys = (jnp.where(keep2, e, n_exp) << bits) | jnp.arange(e.shape[0], dtype=jnp.int32)
    order = jnp.sort(keys) & ((1 << bits) - 1)
    row_expert = jnp.repeat(tile_expert, tm)
    k_in = jnp.arange(rows, dtype=jnp.int32) - off[row_expert]
    first = jnp.cumsum(counts) - counts
    slot = jnp.clip(first[row_expert] + k_in, 0, e.shape[0] - 1)
    src = jnp.where(k_in < counts[row_expert], order[slot] // N_TOP, 0).astype(jnp.int32)
    return dst, src, tile_expert, tile_end[-1:].astype(jnp.int32)


def kernel(x, c, ctx, c_ctx, w_mod, b_mod, w_in, lam_q1, lam_k1, lam_q2, lam_k2, subln_g,
           conv_w, conv_b, rg_wa, rg_ba, rg_wx, rg_bx, rg_lambda, w_out,
           ln1_g, ln1_b, ln2_g, ln2_b, ffn_w1, ffn_w3, ffn_w2,
           moe_router, moe_w1, moe_w3, moe_w2):
    b, seq, d = x.shape
    n_ctx = ctx.shape[1]
    t = n_ctx + seq
    depth = w_in.shape[0]
    alpha = (2 * depth) ** 0.25
    assert seq % GRID_W == 0 and d - ATTN_W == ATTN_W and t % n_ctx == 0

    tm = _pick_tile(t, 768)
    tf = _pick_tile(t, 768)
    te = 512 if (N_TOP * b * t) % 512 == 0 else _pick_tile(N_TOP * b * t, 512)
    tc = _pick_tile(n_ctx, 256)
    chunk = math.gcd(n_ctx, 256)

    h = jnp.concatenate([ctx, x], axis=1)
    mrows = -(-(b + 1) // SUBLANES) * SUBLANES
    cvec = jnp.zeros((mrows, d), F32).at[:b].set(c).at[b].set(c_ctx)
    mod_all = _modulation(cvec, w_mod, b_mod).reshape(depth, mrows, 6, d)
    cs_tab, sn_tab = _rope_tables(n_ctx, seq)

    w_in_p = _prep_w_in(w_in)
    w_out_b = w_out.astype(BF16)
    moe_w1b, moe_w3b, moe_w2b = (w.astype(BF16) for w in (moe_w1, moe_w3, moe_w2))
    lamv =jnp.stack([lam_q1, lam_k1, lam_q2, lam_k2], axis=1)
    rg_bias = jnp.stack([rg_ba, rg_bx], axis=2).reshape(depth, 4, -1)
    is_lat = jnp.tile(jnp.arange(t) >= n_ctx, b)

    for i in range(depth):
        last = i == depth - 1
        lam_init = 0.8 - 0.6 * math.exp(-0.3 * i)
        mod = mod_all[i]
        g2_ln = (ln2_g[i][None, :], ln2_b[i][None, :])
        k, v, xr, q, y = _in_proj(h, mod, w_in_p[i], cs_tab, sn_tab, n_ctx, tm)
        att = _attention(q, k, v, lamv[i], subln_g[i][None, :], n_ctx, lam_init)
        rg = _rglru(xr, y, conv_w[i], conv_b[i][None, :], _prep_rg_gates(rg_wa[i], rg_wx[i]),
                    rg_bias[i], rg_lambda[i], n_ctx, chunk)
        j = i // 2
        router = moe_router[j] if i % 2 else None
        outs = _out_proj(att, rg, h, mod, w_out_b[i], ln1_g[i][None, :], ln1_b[i][None, :],
                         router, n_ctx, tm, alpha)
        h1 = outs[0]
        if i % 2 == 0:
            h = _dense_ffn(h1, mod, ffn_w1[j][None].astype(BF16), ffn_w3[j][None].astype(BF16),
                           ffn_w2[j][None].astype(BF16), *g2_ln, n_ctx, tf, alpha)
        else:
            u, route = outs[1], outs[2]
            n_exp = route.shape[1]
            keep = is_lat if last else jnp.ones((b * t,), bool)
            dst, src, tile_expert, n_used = _routing_tables(route, keep, n_exp, te)
            ys = _moe_ffn(u.reshape(b * t, d), src, moe_w1b, moe_w3b, moe_w2b, j,
                          tile_expert, n_used, te)
            h = _moe_combine(h1, mod, *g2_ln, ys, dst, route, n_ctx, tc, alpha, skip_ctx=last)
    return h if h.shape[1] == seq else h[:, n_ctx:, :]
```

```python
import functools
import math

import jax
import jax.numpy as jnp
from jax import lax
from jax.experimental import pallas as pl
from jax.experimental.pallas import tpu as pltpu

F32 = jnp.float32
BF16 = jnp.bfloat16

HEAD_DIM = 64
ATTN_HEADS = 4
HEAD_W = 2 * HEAD_DIM
ATTN_W = ATTN_HEADS * HEAD_W
GRID_W = 64
RG_BLOCKS = 8
RG_C = 8.0
CONV_W = 4
N_TOP = 2
ROPE_BASE = 10000.0
LN_EPS = 1e-5
RMS_EPS = 1e-5

LANES = 128
SUBLANES = 8
MXU_DIM = 256
VMEM_LIMIT = 56 * 1024 * 1024


def _cparams(sem, vmem=VMEM_LIMIT):
    return pltpu.CompilerParams(dimension_semantics=sem, vmem_limit_bytes=vmem)


def _pick_tile(n, pref):
    t = min(n, pref)
    while t > 8 and (n % t or t % 8):
        t -= 8
    assert n % t == 0
    return t


def _layer_norm(z, g, b):
    mu = jnp.mean(z, -1, keepdims=True)
    zc = z - mu
    var = jnp.mean(zc * zc, -1, keepdims=True)
    return zc * lax.rsqrt(var + LN_EPS) * g + b


def _row_mods(mb_ref, mc_ref, idx, is_ctx):
    return jnp.where(is_ctx, mc_ref[0, idx:idx + 1, :], mb_ref[0, idx:idx + 1, :])


def _sigmoid(x):
    return 0.5 * jnp.tanh(0.5 * x) + 0.5


def _mod_kernel(c_ref, w_ref, b_ref, o_ref):
    cv = c_ref[...]
    s = (cv * _sigmoid(cv)).astype(BF16)
    o_ref[0] = jnp.dot(s, w_ref[0].astype(BF16), preferred_element_type=F32) + b_ref[0]


def _modulation(cvec, w_mod, b_mod):
    depth, d, n = w_mod.shape
    mb = cvec.shape[0]
    tn = _pick_tile(n, 1536)
    return pl.pallas_call(
        _mod_kernel,
        grid=(depth, n // tn),
        in_specs=[
            pl.BlockSpec((mb, d), lambda i, j: (0, 0)),
            pl.BlockSpec((1, d, tn), lambda i, j: (i, 0, j)),
            pl.BlockSpec((1, 1, tn), lambda i, j: (i, 0, j)),
        ],
        out_specs=pl.BlockSpec((1, mb, tn), lambda i, j: (i, 0, j)),
        out_shape=jax.ShapeDtypeStruct((depth, mb, n), F32),
        compiler_params=_cparams(("arbitrary", "arbitrary")),
        name="adaln_mod",
    )(cvec, w_mod, b_mod.reshape(depth, 1, n))


def _rope_store(t, cs, sn, out_ref):
    lane = lax.broadcasted_iota(jnp.int32, (1, LANES), 1)
    first = (lane % HEAD_DIM) < (HEAD_DIM // 2)
    for g in range(ATTN_W // LANES):
        tg = t[:, g * LANES:(g + 1) * LANES]
        sw = jnp.where(first, pltpu.roll(tg, LANES - HEAD_DIM // 2, 1),
                       pltpu.roll(tg, HEAD_DIM // 2, 1))
        out_ref[0, :, g * LANES:(g + 1) * LANES] = (tg * cs + sw * sn).astype(out_ref.dtype)


def _in_kernel(h_ref, mb_ref, mc_ref, w_ref, cs_ref, sn_ref,
               k_ref, v_ref, xr_ref, q_ref, y_ref, *, tm, n_ctx):
    j = pl.program_id(1)
    row = j * tm + lax.broadcasted_iota(jnp.int32, (tm, 1), 0)
    is_ctx = row < n_ctx
    shift = _row_mods(mb_ref, mc_ref, 0, is_ctx)
    scale = _row_mods(mb_ref, mc_ref, 1, is_ctx)
    x = (h_ref[0] * (1.0 + scale) + shift).astype(BF16)
    cs = cs_ref[...]
    sn = sn_ref[...]
    w = ATTN_W

    def proj(part):
        return jnp.dot(x, w_ref[:, part * w:(part + 1) * w], preferred_element_type=F32)

    _rope_store(proj(0), cs, sn, k_ref)
    v_ref[0] = proj(1).astype(v_ref.dtype)
    xr_ref[0] = proj(2)
    _rope_store(proj(3), cs, sn, q_ref)
    y_ref[0] = proj(4)


def _in_proj(h, mod, w_in, cs_tab, sn_tab, n_ctx, tm):
    b, t, d = h.shape
    n = w_in.shape[1]
    w = ATTN_W
    assert n == 5 * w
    row_spec = lambda width: pl.BlockSpec((1, tm, width), lambda i, j: (i, j, 0))
    return pl.pallas_call(
        functools.partial(_in_kernel, tm=tm, n_ctx=n_ctx),
        grid=(b, t // tm),
        in_specs=[
            row_spec(d),
            pl.BlockSpec((1, 6, d), lambda i, j: (i, 0, 0)),
            pl.BlockSpec((1, 6, d), lambda i, j: (b, 0, 0)),
            pl.BlockSpec((d, n), lambda i, j: (0, 0)),
            pl.BlockSpec((tm, LANES), lambda i, j: (j, 0)),
            pl.BlockSpec((tm, LANES), lambda i, j: (j, 0)),
        ],
        out_specs=[row_spec(w)] * 5,
        out_shape=[
            jax.ShapeDtypeStruct((b, t, w), BF16),
            jax.ShapeDtypeStruct((b, t, w), BF16),
            jax.ShapeDtypeStruct((b, t, w), F32),
            jax.ShapeDtypeStruct((b, t, w), BF16),
            jax.ShapeDtypeStruct((b, t, w), F32),
        ],
        compiler_params=_cparams(("arbitrary", "arbitrary")),
        name="in_proj",
    )(h, mod, mod, w_in, cs_tab, sn_tab)


def _attn_kernel(q_ref, k_ref, v_ref, lamv_ref, g_ref, o_ref, *, n_ctx, n_all, lam_init):
    j = pl.program_id(1)
    lv = lamv_ref[...]
    lam = (jnp.exp(jnp.sum(lv[0:1] * lv[1:2], keepdims=True))
           - jnp.exp(jnp.sum(lv[2:3] * lv[3:4], keepdims=True)) + lam_init)
    gain = g_ref[...] * (1.0 - lam_init)
    lane = lax.broadcasted_iota(jnp.int32, (1, HEAD_W), 1)
    dn = (((1,), (1,)), ((), ()))

    def softmax_av(qm, kh, vext):
        s = lax.dot_general(qm, kh, dn, preferred_element_type=F32)
        p = jnp.exp2(s - jnp.max(s, -1, keepdims=True)).astype(BF16)
        oe = jnp.dot(p, vext, preferred_element_type=F32)
        return oe[:, :HEAD_W] / oe[:, HEAD_W:HEAD_W + 1]

    def run(nk):
        ones = jnp.ones((nk, HEAD_W), BF16)
        for h in range(ATTN_HEADS):
            sl = slice(h * HEAD_W, (h + 1) * HEAD_W)
            qh = q_ref[0, :, sl]
            kh = k_ref[0, 0:nk, sl]
            vext = jnp.concatenate([v_ref[0, 0:nk, sl], ones], axis=1)
            q1 = jnp.where(lane < HEAD_DIM, qh, jnp.zeros_like(qh))
            q2 = jnp.where(lane >= HEAD_DIM, qh, jnp.zeros_like(qh))
            s1 = lax.dot_general(q1, kh, dn, preferred_element_type=F32)
            s2 = lax.dot_general(q2, kh, dn, preferred_element_type=F32)
            p1 = jnp.exp2(s1 - jnp.max(s1, -1, keepdims=True)).astype(BF16)
            p2 = jnp.exp2(s2 - jnp.max(s2, -1, keepdims=True)).astype(BF16)
            oe1 = jnp.dot(p1, vext, preferred_element_type=F32)
            oe2 = jnp.dot(p2, vext, preferred_element_type=F32)
            o = (oe1[:, :HEAD_W] / oe1[:, HEAD_W:HEAD_W + 1]
                 - lam * (oe2[:, :HEAD_W] / oe2[:, HEAD_W:HEAD_W + 1]))
            o = o * lax.rsqrt(jnp.mean(o * o, -1, keepdims=True) + RMS_EPS) * gain
            o_ref[0, :, sl] = o.astype(o_ref.dtype)

    def run_small(nk):
        ones = jnp.ones((nk, HEAD_W), BF16)
        sls = [slice(h * HEAD_W, (h + 1) * HEAD_W) for h in range(ATTN_HEADS)]
        masks = (lane < HEAD_DIM, lane >= HEAD_DIM)
        ss = []
        for sl in sls:
            qh = q_ref[0, :, sl]
            kh = k_ref[0, 0:nk, sl]
            for mk in masks:
                ss.append(lax.dot_general(jnp.where(mk, qh, jnp.zeros_like(qh)), kh, dn,
                                          preferred_element_type=F32))
        mx = [jnp.max(s, -1, keepdims=True) for s in ss]
        ps = [jnp.exp2(s - m).astype(BF16) for s, m in zip(ss, mx)]
        vexts = [jnp.concatenate([v_ref[0, 0:nk, sl], ones], axis=1) for sl in sls]
        oes = [jnp.dot(p, vexts[n // 2], preferred_element_type=F32) for n, p in enumerate(ps)]
        ons = [oe[:, :HEAD_W] / oe[:, HEAD_W:HEAD_W + 1] for oe in oes]
        os_ = [ons[2 * h] - lam * ons[2 * h + 1] for h in range(ATTN_HEADS)]
        ms = [jnp.mean(o * o, -1, keepdims=True) for o in os_]
        for h, sl in enumerate(sls):
            o_ref[0, :, sl] = (os_[h] * lax.rsqrt(ms[h] + RMS_EPS) * gain).astype(o_ref.dtype)

    @pl.when(j == 0)
    def _():
        run_small(n_ctx)

    @pl.when(j > 0)
    def _():
        run(n_all)


def _attention(q, k, v, lamv, gain, n_ctx, lam_init):
    b, t, w = q.shape
    tq = n_ctx
    return pl.pallas_call(
        functools.partial(_attn_kernel, n_ctx=n_ctx, n_all=t, lam_init=lam_init),
        grid=(b, t // tq),
        in_specs=[
            pl.BlockSpec((1, tq, w), lambda i, j: (i, j, 0)),
            pl.BlockSpec((1, t, w), lambda i, j: (i, 0, 0)),
            pl.BlockSpec((1, t, w), lambda i, j: (i, 0, 0)),
            pl.BlockSpec((4, HEAD_DIM), lambda i, j: (0, 0)),
            pl.BlockSpec((1, HEAD_W), lambda i, j: (0, 0)),
        ],
        out_specs=pl.BlockSpec((1, tq, w), lambda i, j: (i, j, 0)),
        out_shape=jax.ShapeDtypeStruct((b, t, w), BF16),
        compiler_params=_cparams(("arbitrary", "arbitrary")),
        name="diff_attn",
    )(q, k, v, lamv, gain)


def _gelu_tanh(x):
    return 0.5 * x * (1.0 + jnp.tanh(math.sqrt(2.0 / math.pi) * (x + 0.044715 * x * x * x)))


def _softplus(z):
    return jnp.maximum(z, 0.0) + jnp.log1p(jnp.exp(-jnp.abs(z)))


def _rg_kernel(xr_ref, y_ref, cw_ref, cb_ref, wg_ref, bias_ref, lam_ref, o_ref,
               xc_sc, acc_sc, a_sl, b_sl, *, n_ctx, n_all, chunk):
    t, c, r = n_all, n_ctx, chunk
    half = MXU_DIM
    rgw = xc_sc.shape[1]
    groups = rgw // LANES
    nseg = SUBLANES
    seg = r // nseg

    row = lax.broadcasted_iota(jnp.int32, (t, 1), 0)
    in_ctx = row < c
    tl = jnp.where(in_ctx, row, row - c)
    span = jnp.where(in_ctx, c, t - c)
    for g in range(groups):
        sl = slice(g * LANES, (g + 1) * LANES)
        xg = xr_ref[0, :, sl]
        wv = cw_ref[:, sl]
        acc = cb_ref[:, sl] + xg * wv[2:3]
        acc = acc + jnp.where(tl >= 2, pltpu.roll(xg, 2, 0), 0.0) * wv[0:1]
        acc = acc + jnp.where(tl >= 1, pltpu.roll(xg, 1, 0), 0.0) * wv[1:2]
        acc = acc + jnp.where(tl + 1 < span, pltpu.roll(xg, t - 1, 0), 0.0) * wv[3:4]
        xc_sc[:, sl] = acc

    n_chunks = t // r
    ctx_chunks = c // r
    for d in range(2):
        c4 = (-0.5 * RG_C) * _softplus(-lam_ref[d:d + 1, :])
        ba = 0.5 * bias_ref[2 * d:2 * d + 1, :]
        bx = 0.5 * bias_ref[2 * d + 1:2 * d + 2, :]

        def chunk_step(s, hc, d=d, c4=c4, ba=ba, bx=bx):
            if d == 0:
                ci = s
            else:
                ci = jnp.where(s < ctx_chunks, ctx_chunks - 1 - s,
                               n_chunks - 1 - (s - ctx_chunks))
            r0 = pl.multiple_of(ci * r, r)
            xc = xc_sc[pl.ds(r0, r), :]
            xb = xc.astype(BF16)
            for hh in range(rgw // half):
                hs = slice(hh * half, (hh + 1) * half)
                g2 = jnp.dot(xb[:, hs], wg_ref[d, hh], preferred_element_type=F32)
                tr = jnp.tanh(g2[:, :half] + ba[:, hs])
                gi = 0.5 * jnp.tanh(g2[:, half:] + bx[:, hs]) + 0.5
                log_a = c4[:, hs] * (tr + 1.0)
                a = jnp.exp(log_a)
                th = jnp.tanh(log_a)
                bt = jnp.sqrt(-2.0 * th) * lax.rsqrt(1.0 - th) * (gi * xc[:, hs])
                for gg in range(half // LANES):
                    g = hh * (half // LANES) + gg
                    ls = slice(gg * LANES, (gg + 1) * LANES)
                    for k in range(nseg):
                        rows = slice(k * seg, (k + 1) * seg)
                        a_sl[g, pl.ds(k, seg, stride=nseg), :] = a[rows, ls]
                        b_sl[g, pl.ds(k, seg, stride=nseg), :] = bt[rows, ls]

            def seg_step(ii, carry):
                i = ii if d == 0 else seg - 1 - ii
                base = pl.multiple_of(i * nseg, nseg)
                hs_, as_ = carry
                nh, na = [], []
                for g in range(groups):
                    av = a_sl[g, pl.ds(base, nseg), :]
                    hv = av * hs_[g] + b_sl[g, pl.ds(base, nseg), :]
                    pv = av * as_[g]
                    b_sl[g, pl.ds(base, nseg), :] = hv
                    a_sl[g, pl.ds(base, nseg), :] = pv
                    nh.append(hv)
                    na.append(pv)
                return tuple(nh), tuple(na)

            zero = jnp.zeros((nseg, LANES), F32)
            one = jnp.ones((nseg, LANES), F32)
            h_end, a_end = lax.fori_loop(
                0, seg, seg_step, ((zero,) * groups, (one,) * groups), unroll=4)

            order = range(nseg) if d == 0 else range(nseg - 1, -1, -1)
            new_carry = []
            for g in range(groups):
                ls = slice(g * LANES, (g + 1) * LANES)
                h_in = hc[:, ls]
                for k in order:
                    rows = pl.ds(r0 + k * seg, seg)
                    h_true = (b_sl[g, pl.ds(k, seg, stride=nseg), :]
                              + a_sl[g, pl.ds(k, seg, stride=nseg), :] * h_in)
                    if d == 0:
                        acc_sc[rows, ls] = h_true
                    else:
                        tot = acc_sc[rows, ls] + h_true
                        o_ref[0, rows, ls] = (tot * _gelu_tanh(y_ref[0, rows, ls])).astype(o_ref.dtype)
                    h_in = h_end[g][k:k + 1, :] + a_end[g][k:k + 1, :] * h_in
                new_carry.append(h_in)
            return jnp.concatenate(new_carry, axis=1)

        lax.fori_loop(0, n_chunks, chunk_step, jnp.zeros((1, rgw), F32))


def _rglru(xr, y, conv_w, conv_b, wg, bias, lam, n_ctx, chunk):
    b, t, w = xr.shape
    full = lambda a: pl.BlockSpec(a.shape, lambda i: (0,) * a.ndim)
    seq = pl.BlockSpec((1, t, w), lambda i: (i, 0, 0))
    return pl.pallas_call(
        functools.partial(_rg_kernel, n_ctx=n_ctx, n_all=t, chunk=chunk),
        grid=(b,),
        in_specs=[seq, seq, full(conv_w), full(conv_b), full(wg), full(bias), full(lam)],
        out_specs=seq,
        out_shape=jax.ShapeDtypeStruct((b, t, w), BF16),
        scratch_shapes=[
            pltpu.VMEM((t, w), F32),
            pltpu.VMEM((t, w), F32),
            pltpu.VMEM((w // LANES, chunk, LANES), F32),
            pltpu.VMEM((w // LANES, chunk, LANES), F32),
        ],
        compiler_params=_cparams(("arbitrary",)),
        name="rglru",
    )(xr, y, conv_w, conv_b, wg, bias, lam)


def _out_kernel(*refs, tm, n_ctx, alpha, moe):
    if moe:
        (att_ref, rg_ref, h_ref, mb_ref, mc_ref, w_ref, g_ref, b_ref, rt_ref,
         h1_ref, u_ref, route_ref) = refs
    else:
        att_ref, rg_ref, h_ref, mb_ref, mc_ref, w_ref, g_ref, b_ref, h1_ref = refs
    j = pl.program_id(1)
    row = j * tm + lax.broadcasted_iota(jnp.int32, (tm, 1), 0)
    is_ctx = row < n_ctx
    aw = att_ref.shape[2]
    mix = (jnp.dot(att_ref[0], w_ref[0:aw, :], preferred_element_type=F32)
           + jnp.dot(rg_ref[0], w_ref[aw:, :], preferred_element_type=F32))
    g1 = _row_mods(mb_ref, mc_ref, 2, is_ctx)
    h1 = _layer_norm(alpha * h_ref[0] + (1.0 + g1) * mix, g_ref[...], b_ref[...])
    h1_ref[0] = h1
    if not moe:
        return
    sh2 = _row_mods(mb_ref, mc_ref, 3, is_ctx)
    sc2 = _row_mods(mb_ref, mc_ref, 4, is_ctx)
    u = h1 * (1.0 + sc2) + sh2
    u_ref[0] = u
    n_exp = route_ref.shape[1]
    u_hi = u.astype(BF16)
    u_lo = (u - u_hi.astype(F32)).astype(BF16)
    prod = (jnp.dot(u_hi, rt_ref[...], preferred_element_type=F32)
            + jnp.dot(u_lo, rt_ref[...], preferred_element_type=F32))
    logits = prod[:, :LANES] + prod[:, LANES:]
    lane = lax.broadcasted_iota(jnp.int32, (1, LANES), 1)
    neg = -jnp.inf
    lg = jnp.where(lane < n_exp, logits, neg)
    m1 = jnp.max(lg, -1, keepdims=True)
    i1 = jnp.min(jnp.where(lg == m1, lane, LANES), -1, keepdims=True)
    lg2 = jnp.where(lane == i1, neg, lg)
    m2 = jnp.max(lg2, -1, keepdims=True)
    i2 = jnp.min(jnp.where(lg2 == m2, lane, LANES), -1, keepdims=True)
    e2 = jnp.exp(m2 - m1)
    den = 1.0 + e2
    route = jnp.where(lane == 0, i1.astype(F32),
                      jnp.where(lane == 1, i2.astype(F32),
                                jnp.where(lane == 2, 1.0 / den,
                                          jnp.where(lane == 3, e2 / den, 0.0))))
    route_ref[...] = route[:, :n_exp]


def _out_proj(att, rg, h, mod, w_out, ln_g, ln_b, router, n_ctx, tm, alpha):
    b, t, d = h.shape
    aw = att.shape[2]
    moe = router is not None
    row_spec = lambda width: pl.BlockSpec((1, tm, width), lambda i, j: (i, j, 0))
    vec = pl.BlockSpec((1, d), lambda i, j: (0, 0))
    in_specs = [
        row_spec(aw), row_spec(aw), row_spec(d),
        pl.BlockSpec((1, 6, d), lambda i, j: (i, 0, 0)),
        pl.BlockSpec((1, 6, d), lambda i, j: (b, 0, 0)),
        pl.BlockSpec((d, d), lambda i, j: (0, 0)),
        vec, vec,
    ]
    args = [att, rg, h, mod, mod, w_out, ln_g, ln_b]
    out_specs = [row_spec(d)]
    out_shape = [jax.ShapeDtypeStruct((b, t, d), F32)]
    if moe:
        n_exp = router.shape[1]
        r_hi = router.astype(BF16)
        r_lo = (router - r_hi.astype(F32)).astype(BF16)
        pad = lambda a: jnp.zeros((d, LANES), BF16).at[:, :n_exp].set(a)
        in_specs.append(pl.BlockSpec((d, 2 * LANES), lambda i, j: (0, 0)))
        args.append(jnp.concatenate([pad(r_hi), pad(r_lo)], axis=1))
        tiles = t // tm
        out_specs += [row_spec(d), pl.BlockSpec((tm, n_exp), lambda i, j: (i * tiles + j, 0))]
        out_shape += [jax.ShapeDtypeStruct((b, t, d), F32),
                      jax.ShapeDtypeStruct((b * t, n_exp), F32)]
    return pl.pallas_call(
        functools.partial(_out_kernel, tm=tm, n_ctx=n_ctx, alpha=alpha, moe=moe),
        grid=(b, t // tm),
        in_specs=in_specs,
        out_specs=out_specs,
        out_shape=out_shape,
        compiler_params=_cparams(("arbitrary", "arbitrary")),
        name="out_proj_moe" if moe else "out_proj",
    )(*args)


def _swiglu(x, w1_ref, w3_ref, w2_ref, fc):
    f = w1_ref.shape[2]
    acc = None
    for c in range(f // fc):
        cs = slice(c * fc, (c + 1) * fc)
        a = jnp.dot(x, w1_ref[0, :, cs], preferred_element_type=F32)
        bgate = jnp.dot(x, w3_ref[0, :, cs], preferred_element_type=F32)
        gact = (a * _sigmoid(a) * bgate).astype(BF16)
        part = jnp.dot(gact, w2_ref[0, cs, :], preferred_element_type=F32)
        acc = part if acc is None else acc + part
    return acc


def _ffn_chunk(f):
    return MXU_DIM if f % MXU_DIM == 0 else f


def _dense_ffn_kernel(h1_ref, mb_ref, mc_ref, w1_ref, w3_ref, w2_ref, g_ref, b_ref, o_ref,
                      *, tf, n_ctx, alpha, fc):
    row = pl.program_id(1) * tf + lax.broadcasted_iota(jnp.int32, (tf, 1), 0)
    is_ctx = row < n_ctx
    sh2 = _row_mods(mb_ref, mc_ref, 3, is_ctx)
    sc2 = _row_mods(mb_ref, mc_ref, 4, is_ctx)
    g2 = _row_mods(mb_ref, mc_ref, 5, is_ctx)
    h1 = h1_ref[0]
    u = (h1 * (1.0 + sc2) + sh2).astype(BF16)
    mixed = _swiglu(u, w1_ref, w3_ref, w2_ref, fc)
    o_ref[0] = _layer_norm(alpha * h1 + (1.0 + g2) * mixed, g_ref[...], b_ref[...])


def _dense_ffn(h1, mod, w1, w3, w2, ln_g, ln_b, n_ctx, tf, alpha):
    b, t, d = h1.shape
    f = w1.shape[2]
    row_spec = pl.BlockSpec((1, tf, d), lambda i, j: (i, j, 0))
    vec = pl.BlockSpec((1, d), lambda i, j: (0, 0))
    const3 = lambda a: pl.BlockSpec(a.shape, lambda i, j: (0, 0, 0))
    return pl.pallas_call(
        functools.partial(_dense_ffn_kernel, tf=tf, n_ctx=n_ctx, alpha=alpha, fc=_ffn_chunk(f)),
        grid=(b, t // tf),
        in_specs=[
            row_spec,
            pl.BlockSpec((1, 6, d), lambda i, j: (i, 0, 0)),
            pl.BlockSpec((1, 6, d), lambda i, j: (b, 0, 0)),
            const3(w1), const3(w3), const3(w2), vec, vec,
        ],
        out_specs=row_spec,
        out_shape=jax.ShapeDtypeStruct((b, t, d), F32),
        compiler_params=_cparams(("arbitrary", "arbitrary")),
        name="dense_ffn",
    )(h1, mod, mod, w1, w3, w2, ln_g, ln_b)


def _moe_ffn_kernel(te_ref, nu_ref, src_ref, nxt_ref, u_ref, w1_ref, w3_ref, w2_ref, o_ref,
                    xbuf, sem, *, tm, fc):
    del te_ref
    i = pl.program_id(0)
    n_used = nu_ref[0]
    slot = i % 2

    def gather(idx_ref, dst_slot):
        for r in range(tm):
            pltpu.make_async_copy(u_ref.at[pl.ds(idx_ref[0, 0, r], 1)],
                                  xbuf.at[dst_slot, pl.ds(r, 1)], sem.at[dst_slot]).start()

    @pl.when(jnp.logical_and(i == 0, n_used > 0))
    def _():
        gather(src_ref, 0)

    @pl.when(i + 1 < n_used)
    def _():
        gather(nxt_ref, 1 - slot)

    @pl.when(i < n_used)
    def _():
        pltpu.make_async_copy(u_ref.at[pl.ds(0, tm)], xbuf.at[slot], sem.at[slot]).wait()
        o_ref[...] = _swiglu(xbuf[slot].astype(BF16), w1_ref, w3_ref, w2_ref, fc)

    @pl.when(i >= n_used)
    def _():
        o_ref[...] = jnp.zeros_like(o_ref)


def _moe_ffn(u, src, w1, w3, w2, layer, tile_expert, n_used, tm):
    n, d = u.shape
    f = w1.shape[3]
    n_tiles = src.shape[0] // tm
    src3 = src.reshape(n_tiles, 1, tm)
    wspec = lambda shape: pl.BlockSpec((None,) + shape, lambda i, te, nu: (layer, te[i], 0, 0))
    grid_spec = pltpu.PrefetchScalarGridSpec(
        num_scalar_prefetch=2,
        grid=(n_tiles,),
        in_specs=[
            pl.BlockSpec((1, 1, tm), lambda i, te, nu: (i, 0, 0), memory_space=pltpu.SMEM),
            pl.BlockSpec((1, 1, tm), lambda i, te, nu: (jnp.minimum(i + 1, n_tiles - 1), 0, 0),
                         memory_space=pltpu.SMEM),
            pl.BlockSpec(memory_space=pl.ANY),
            wspec((1, d, f)), wspec((1, d, f)), wspec((1, f, d)),
        ],
        out_specs=pl.BlockSpec((tm, d), lambda i, te, nu: (i, 0)),
        scratch_shapes=[pltpu.VMEM((2, tm, d), F32), pltpu.SemaphoreType.DMA((2,))],
    )
    return pl.pallas_call(
        functools.partial(_moe_ffn_kernel, tm=tm, fc=_ffn_chunk(f)),
        grid_spec=grid_spec,
        out_shape=jax.ShapeDtypeStruct((n_tiles * tm, d), F32),
        compiler_params=_cparams(("arbitrary",)),
        name="moe_ffn",
    )(tile_expert, n_used, src3, src3, u, w1, w3, w2)


def _combine_kernel(dst_ref, nxt_ref, ys_ref, route_ref, h1_ref, mb_ref, mc_ref, g_ref, b_ref,
                    o_ref, buf, sem, *, tc, n_ctx, first_tile, alpha):
    i, j = pl.program_id(0), pl.program_id(1)
    nj = pl.num_programs(1)
    s = i * nj + j
    total = pl.num_programs(0) * nj
    slot = s % 2

    def gather(idx_ref, dslot):
        for r in range(tc):
            for k in range(N_TOP):
                pltpu.make_async_copy(ys_ref.at[pl.ds(idx_ref[0, 0, N_TOP * r + k], 1)],
                                      buf.at[dslot, pl.ds(k * tc + r, 1)], sem.at[dslot]).start()

    @pl.when(s == 0)
    def _():
        gather(dst_ref, 0)

    @pl.when(s + 1 < total)
    def _():
        gather(nxt_ref, 1 - slot)

    pltpu.make_async_copy(ys_ref.at[pl.ds(0, N_TOP * tc)], buf.at[slot], sem.at[slot]).wait()
    rt = route_ref[...]
    mixed = rt[:, 2:3] * buf[slot, 0:tc] + rt[:, 3:4] * buf[slot, tc:N_TOP * tc]
    row = (first_tile + j) * tc + lax.broadcasted_iota(jnp.int32, (tc, 1), 0)
    g2 = _row_mods(mb_ref, mc_ref, 5, row < n_ctx)
    o_ref[0] = _layer_norm(alpha * h1_ref[0] + (1.0 + g2) * mixed, g_ref[...], b_ref[...])


def _moe_combine(h1, mod, ln_g, ln_b, ys, dst, route, n_ctx, tc, alpha, skip_ctx):
    b, t, d = h1.shape
    tiles = t // tc
    off = n_ctx // tc if skip_ctx else 0
    nj = tiles - off
    n_exp = route.shape[1]
    dst3 = dst.reshape(b * tiles, 1, N_TOP * tc)

    def cur(i, j):
        return i * tiles + off + j

    def nxt(i, j):
        i2 = jnp.minimum(i + (j + 1) // nj, b - 1)
        return i2 * tiles + off + (j + 1) % nj

    vec = pl.BlockSpec((1, d), lambda i, j: (0, 0))
    return pl.pallas_call(
        functools.partial(_combine_kernel, tc=tc, n_ctx=n_ctx, first_tile=off, alpha=alpha),
        grid=(b, nj),
        in_specs=[
            pl.BlockSpec((1, 1, N_TOP * tc), lambda i, j: (cur(i, j), 0, 0),
                         memory_space=pltpu.SMEM),
            pl.BlockSpec((1, 1, N_TOP * tc), lambda i, j: (nxt(i, j), 0, 0),
                         memory_space=pltpu.SMEM),
            pl.BlockSpec(memory_space=pl.ANY),
            pl.BlockSpec((tc, n_exp), lambda i, j: (cur(i, j), 0)),
            pl.BlockSpec((1, tc, d), lambda i, j: (i, off + j, 0)),
            pl.BlockSpec((1, 6, d), lambda i, j: (i, 0, 0)),
            pl.BlockSpec((1, 6, d), lambda i, j: (b, 0, 0)),
            vec, vec,
        ],
        out_specs=pl.BlockSpec((1, tc, d), lambda i, j: (i, j, 0)),
        out_shape=jax.ShapeDtypeStruct((b, nj * tc, d), F32),
        scratch_shapes=[pltpu.VMEM((2, N_TOP * tc, d), F32), pltpu.SemaphoreType.DMA((2,))],
        compiler_params=_cparams(("arbitrary", "arbitrary")),
        name="moe_combine",
    )(dst3, dst3, ys, route, h1, mod, mod, ln_g, ln_b)


def _rope_tables(n_ctx, seq):
    rows = seq // GRID_W
    pairs = HEAD_DIM // 4
    rpos = jnp.repeat(jnp.arange(rows, dtype=F32), GRID_W)
    cpos = jnp.tile(jnp.arange(GRID_W, dtype=F32), rows)
    inv_freq = ROPE_BASE ** (-jnp.arange(pairs, dtype=F32) / pairs)
    ang = jnp.concatenate([rpos[:, None] * inv_freq, cpos[:, None] * inv_freq], -1)
    cos, sin = jnp.cos(ang), jnp.sin(ang)
    reps = LANES // HEAD_DIM
    cs = jnp.tile(jnp.concatenate([cos, cos], -1), (1, reps))
    sn = jnp.tile(jnp.concatenate([-sin, sin], -1), (1, reps))
    cs = jnp.concatenate([jnp.ones((n_ctx, LANES), F32), cs], 0)
    sn = jnp.concatenate([jnp.zeros((n_ctx, LANES), F32), sn], 0)
    return cs, sn


def _prep_w_in(w_in):
    lead = w_in.shape[:-1]
    q0 = 3 * ATTN_W

    def split_pairs(w):
        w = w.reshape(*lead, ATTN_W // HEAD_DIM, HEAD_DIM // 2, 2)
        return jnp.swapaxes(w, -1, -2).reshape(*lead, ATTN_W)

    q_scale = HEAD_DIM ** -0.5 * math.log2(math.e)
    return jnp.concatenate([
        split_pairs(w_in[..., :ATTN_W]), w_in[..., ATTN_W:q0],
        split_pairs(w_in[..., q0:q0 + ATTN_W]) * q_scale, w_in[..., q0 + ATTN_W:]],
        axis=-1).astype(BF16)


def _prep_rg_gates(wa, wx):
    nd, g, bw, _ = wa.shape
    per = MXU_DIM // bw
    halves = g // per

    def bdiag(w):
        w = w.reshape(nd, halves, per, bw, bw)
        eye = jnp.eye(per, dtype=w.dtype)
        full = w[:, :, :, :, None, :] * eye[None, None, :, None, :, None]
        return full.reshape(nd, halves, per * bw, per * bw)

    return (0.5 * jnp.concatenate([bdiag(wa), bdiag(wx)], -1)).astype(BF16)


def _routing_tables(route, keep, n_exp, tm):
    e = route[:, :N_TOP].astype(jnp.int32).reshape(-1)
    keep2 = jnp.repeat(keep, N_TOP)
    oh = ((e[:, None] == jnp.arange(n_exp)[None, :]) & keep2[:, None]).astype(jnp.int32)
    csum = jnp.cumsum(oh, axis=0)
    rank = jnp.sum((csum - oh) * oh, axis=1)
    counts = csum[-1]
    tiles_per = (counts + tm - 1) // tm
    tile_end = jnp.cumsum(tiles_per)
    off = (tile_end - tiles_per) * tm
    n_tiles = e.shape[0] // tm + n_exp
    rows = n_tiles * tm
    dst = jnp.where(keep2, jnp.sum(oh * off[None, :], axis=1) + rank, rows).astype(jnp.int32)
    tile_ids = jnp.arange(n_tiles)
    tile_expert = jnp.sum((tile_ids[:, None] >= tile_end[None, :]).astype(jnp.int32), axis=1)
    tile_expert = jnp.minimum(tile_expert, n_exp - 1).astype(jnp.int32)
    bits = (e.shape[0] - 1).bit_length()
    keys = (jnp.where(keep2, e, n_exp) << bits) | jnp.arange(e.shape[0], dtype=jnp.int32)
    order = jnp.sort(keys) & ((1 << bits) - 1)
    row_expert = jnp.repeat(tile_expert, tm)
    k_in = jnp.arange(rows, dtype=jnp.int32) - off[row_expert]
    first = jnp.cumsum(counts) - counts
    slot = jnp.clip(first[row_expert] + k_in, 0, e.shape[0] - 1)
    src = jnp.where(k_in < counts[row_expert], order[slot] // N_TOP, 0).astype(jnp.int32)
    return dst, src, tile_expert, tile_end[-1:].astype(jnp.int32)


def kernel(x, c, ctx, c_ctx, w_mod, b_mod, w_in, lam_q1, lam_k1, lam_q2, lam_k2, subln_g,
           conv_w, conv_b, rg_wa, rg_ba, rg_wx, rg_bx, rg_lambda, w_out,
           ln1_g, ln1_b, ln2_g, ln2_b, ffn_w1, ffn_w3, ffn_w2,
           moe_router, moe_w1, moe_w3, moe_w2):
    b, seq, d = x.shape
    n_ctx = ctx.shape[1]
    t = n_ctx + seq
    depth = w_in.shape[0]
    alpha = (2 * depth) ** 0.25
    assert seq % GRID_W == 0 and d - ATTN_W == ATTN_W and t % n_ctx == 0

    tm = _pick_tile(t, 768)
    tf = _pick_tile(t, 768)
    te = 512 if (N_TOP * b * t) % 512 == 0 else _pick_tile(N_TOP * b * t, 512)
    tc = _pick_tile(n_ctx, 256)
    chunk = math.gcd(n_ctx, 256)

    h = jnp.concatenate([ctx, x], axis=1)
    mrows = -(-(b + 1) // SUBLANES) * SUBLANES
    cvec = jnp.zeros((mrows, d), F32).at[:b].set(c).at[b].set(c_ctx)
    mod_all = _modulation(cvec, w_mod, b_mod).reshape(depth, mrows, 6, d)
    cs_tab, sn_tab = _rope_tables(n_ctx, seq)

    w_in_p = _prep_w_in(w_in)
    w_out_b = w_out.astype(BF16)
    moe_w1b, moe_w3b, moe_w2b = (w.astype(BF16) for w in (moe_w1, moe_w3, moe_w2))
    lamv =jnp.stack([lam_q1, lam_k1, lam_q2, lam_k2], axis=1)
    rg_bias = jnp.stack([rg_ba, rg_bx], axis=2).reshape(depth, 4, -1)
    is_lat = jnp.tile(jnp.arange(t) >= n_ctx, b)

    for i in range(depth):
        last = i == depth - 1
        lam_init = 0.8 - 0.6 * math.exp(-0.3 * i)
        mod = mod_all[i]
        g2_ln = (ln2_g[i][None, :], ln2_b[i][None, :])
        k, v, xr, q, y = _in_proj(h, mod, w_in_p[i], cs_tab, sn_tab, n_ctx, tm)
        att = _attention(q, k, v, lamv[i], subln_g[i][None, :], n_ctx, lam_init)
        rg = _rglru(xr, y, conv_w[i], conv_b[i][None, :], _prep_rg_gates(rg_wa[i], rg_wx[i]),
                    rg_bias[i], rg_lambda[i], n_ctx, chunk)
        j = i // 2
        router = moe_router[j] if i % 2 else None
        outs = _out_proj(att, rg, h, mod, w_out_b[i], ln1_g[i][None, :], ln1_b[i][None, :],
                         router, n_ctx, tm, alpha)
        h1 = outs[0]
        if i % 2 == 0:
            h = _dense_ffn(h1, mod, ffn_w1[j][None].astype(BF16), ffn_w3[j][None].astype(BF16),
                           ffn_w2[j][None].astype(BF16), *g2_ln, n_ctx, tf, alpha)
        else:
            u, route = outs[1], outs[2]
            n_exp = route.shape[1]
            keep = is_lat if last else jnp.ones((b * t,), bool)
            dst, src, tile_expert, n_used = _routing_tables(route, keep, n_exp, te)
            ys = _moe_ffn(u.reshape(b * t, d), src, moe_w1b, moe_w3b, moe_w2b, j,
                          tile_expert, n_used, te)
            h = _moe_combine(h1, mod, *g2_ln, ys, dst, route, n_ctx, tc, alpha, skip_ctx=last)
    return h if h.shape[1] == seq else h[:, n_ctx:, :]
```

```python
import functools
import math

import jax
import jax.numpy as jnp
from jax import lax
from jax.experimental import pallas as pl
from jax.experimental.pallas import tpu as pltpu

F32 = jnp.float32
BF16 = jnp.bfloat16

HEAD_DIM = 64
ATTN_HEADS = 4
HEAD_W = 2 * HEAD_DIM
ATTN_W = ATTN_HEADS * HEAD_W
GRID_W = 64
RG_BLOCKS = 8
RG_C = 8.0
CONV_W = 4
N_TOP = 2
ROPE_BASE = 10000.0
LN_EPS = 1e-5
RMS_EPS = 1e-5

LANES = 128
SUBLANES = 8
MXU_DIM = 256
VMEM_LIMIT = 56 * 1024 * 1024


def _cparams(sem, vmem=VMEM_LIMIT):
    return pltpu.CompilerParams(dimension_semantics=sem, vmem_limit_bytes=vmem)


def _pick_tile(n, pref):
    t = min(n, pref)
    while t > 8 and (n % t or t % 8):
        t -= 8
    assert n % t == 0
    return t


def _layer_norm(z, g, b):
    mu = jnp.mean(z, -1, keepdims=True)
    zc = z - mu
    var = jnp.mean(zc * zc, -1, keepdims=True)
    return zc * lax.rsqrt(var + LN_EPS) * g + b


def _row_mods(mb_ref, mc_ref, idx, is_ctx):
    return jnp.where(is_ctx, mc_ref[0, idx:idx + 1, :], mb_ref[0, idx:idx + 1, :])


def _sigmoid(x):
    return 0.5 * jnp.tanh(0.5 * x) + 0.5


def _mod_kernel(c_ref, w_ref, b_ref, o_ref):
    cv = c_ref[...]
    s = (cv * _sigmoid(cv)).astype(BF16)
    o_ref[0] = jnp.dot(s, w_ref[0].astype(BF16), preferred_element_type=F32) + b_ref[0]


def _modulation(cvec, w_mod, b_mod):
    depth, d, n = w_mod.shape
    mb = cvec.shape[0]
    tn = _pick_tile(n, 1536)
    return pl.pallas_call(
        _mod_kernel,
        grid=(depth, n // tn),
        in_specs=[
            pl.BlockSpec((mb, d), lambda i, j: (0, 0)),
            pl.BlockSpec((1, d, tn), lambda i, j: (i, 0, j)),
            pl.BlockSpec((1, 1, tn), lambda i, j: (i, 0, j)),
        ],
        out_specs=pl.BlockSpec((1, mb, tn), lambda i, j: (i, 0, j)),
        out_shape=jax.ShapeDtypeStruct((depth, mb, n), F32),
        compiler_params=_cparams(("arbitrary", "arbitrary")),
        name="adaln_mod",
    )(cvec, w_mod, b_mod.reshape(depth, 1, n))


def _rope_store(t, cs, sn, out_ref):
    lane = lax.broadcasted_iota(jnp.int32, (1, LANES), 1)
    first = (lane % HEAD_DIM) < (HEAD_DIM // 2)
    for g in range(ATTN_W // LANES):
        tg = t[:, g * LANES:(g + 1) * LANES]
        sw = jnp.where(first, pltpu.roll(tg, LANES - HEAD_DIM // 2, 1),
                       pltpu.roll(tg, HEAD_DIM // 2, 1))
        out_ref[0, :, g * LANES:(g + 1) * LANES] = (tg * cs + sw * sn).astype(out_ref.dtype)


def _in_kernel(h_ref, mb_ref, mc_ref, w_ref, cs_ref, sn_ref,
               k_ref, v_ref, xr_ref, q_ref, y_ref, *, tm, n_ctx):
    j = pl.program_id(1)
    row = j * tm + lax.broadcasted_iota(jnp.int32, (tm, 1), 0)
    is_ctx = row < n_ctx
    shift = _row_mods(mb_ref, mc_ref, 0, is_ctx)
    scale = _row_mods(mb_ref, mc_ref, 1, is_ctx)
    x = (h_ref[0] * (1.0 + scale) + shift).astype(BF16)
    cs = cs_ref[...]
    sn = sn_ref[...]
    w = ATTN_W

    def proj(part):
        return jnp.dot(x, w_ref[:, part * w:(part + 1) * w], preferred_element_type=F32)

    _rope_store(proj(0), cs, sn, k_ref)
    v_ref[0] = proj(1).astype(v_ref.dtype)
    xr_ref[0] = proj(2)
    _rope_store(proj(3), cs, sn, q_ref)
    y_ref[0] = proj(4)


def _in_proj(h, mod, w_in, cs_tab, sn_tab, n_ctx, tm):
    b, t, d = h.shape
    n = w_in.shape[1]
    w = ATTN_W
    assert n == 5 * w
    row_spec = lambda width: pl.BlockSpec((1, tm, width), lambda i, j: (i, j, 0))
    return pl.pallas_call(
        functools.partial(_in_kernel, tm=tm, n_ctx=n_ctx),
        grid=(b, t // tm),
        in_specs=[
            row_spec(d),
            pl.BlockSpec((1, 6, d), lambda i, j: (i, 0, 0)),
            pl.BlockSpec((1, 6, d), lambda i, j: (b, 0, 0)),
            pl.BlockSpec((d, n), lambda i, j: (0, 0)),
            pl.BlockSpec((tm, LANES), lambda i, j: (j, 0)),
            pl.BlockSpec((tm, LANES), lambda i, j: (j, 0)),
        ],
        out_specs=[row_spec(w)] * 5,
        out_shape=[
            jax.ShapeDtypeStruct((b, t, w), BF16),
            jax.ShapeDtypeStruct((b, t, w), BF16),
            jax.ShapeDtypeStruct((b, t, w), F32),
            jax.ShapeDtypeStruct((b, t, w), BF16),
            jax.ShapeDtypeStruct((b, t, w), F32),
        ],
        compiler_params=_cparams(("arbitrary", "arbitrary")),
        name="in_proj",
    )(h, mod, mod, w_in, cs_tab, sn_tab)


def _attn_kernel(q_ref, k_ref, v_ref, lamv_ref, g_ref, o_ref, *, n_ctx, n_all, lam_init):
    j = pl.program_id(1)
    lv = lamv_ref[...]
    lam = (jnp.exp(jnp.sum(lv[0:1] * lv[1:2], keepdims=True))
           - jnp.exp(jnp.sum(lv[2:3] * lv[3:4], keepdims=True)) + lam_init)
    gain = g_ref[...] * (1.0 - lam_init)
    lane = lax.broadcasted_iota(jnp.int32, (1, HEAD_W), 1)
    dn = (((1,), (1,)), ((), ()))

    def softmax_av(qm, kh, vext):
        s = lax.dot_general(qm, kh, dn, preferred_element_type=F32)
        p = jnp.exp2(s - jnp.max(s, -1, keepdims=True)).astype(BF16)
        oe = jnp.dot(p, vext, preferred_element_type=F32)
        return oe[:, :HEAD_W] / oe[:, HEAD_W:HEAD_W + 1]

    def run(nk):
        ones = jnp.ones((nk, HEAD_W), BF16)
        for h in range(ATTN_HEADS):
            sl = slice(h * HEAD_W, (h + 1) * HEAD_W)
            qh = q_ref[0, :, sl]
            kh = k_ref[0, 0:nk, sl]
            vext = jnp.concatenate([v_ref[0, 0:nk, sl], ones], axis=1)
            q1 = jnp.where(lane < HEAD_DIM, qh, jnp.zeros_like(qh))
            q2 = jnp.where(lane >= HEAD_DIM, qh, jnp.zeros_like(qh))
            o = softmax_av(q1, kh, vext) - lam * softmax_av(q2, kh, vext)
            o = o * lax.rsqrt(jnp.mean(o * o, -1, keepdims=True) + RMS_EPS) * gain
            o_ref[0, :, sl] = o.astype(o_ref.dtype)

    def run_small(nk):
        ones = jnp.ones((nk, HEAD_W), BF16)
        sls = [slice(h * HEAD_W, (h + 1) * HEAD_W) for h in range(ATTN_HEADS)]
        masks = (lane < HEAD_DIM, lane >= HEAD_DIM)
        ss = []
        for sl in sls:
            qh = q_ref[0, :, sl]
            kh = k_ref[0, 0:nk, sl]
            for mk in masks:
                ss.append(lax.dot_general(jnp.where(mk, qh, jnp.zeros_like(qh)), kh, dn,
                                          preferred_element_type=F32))
        mx = [jnp.max(s, -1, keepdims=True) for s in ss]
        ps = [jnp.exp2(s - m).astype(BF16) for s, m in zip(ss, mx)]
        vexts = [jnp.concatenate([v_ref[0, 0:nk, sl], ones], axis=1) for sl in sls]
        oes = [jnp.dot(p, vexts[n // 2], preferred_element_type=F32) for n, p in enumerate(ps)]
        ons = [oe[:, :HEAD_W] / oe[:, HEAD_W:HEAD_W + 1] for oe in oes]
        os_ = [ons[2 * h] - lam * ons[2 * h + 1] for h in range(ATTN_HEADS)]
        ms = [jnp.mean(o * o, -1, keepdims=True) for o in os_]
        for h, sl in enumerate(sls):
            o_ref[0, :, sl] = (os_[h] * lax.rsqrt(ms[h] + RMS_EPS) * gain).astype(o_ref.dtype)

    @pl.when(j == 0)
    def _():
        run_small(n_ctx)

    @pl.when(j > 0)
    def _():
        run(n_all)


def _attention(q, k, v, lamv, gain, n_ctx, lam_init):
    b, t, w = q.shape
    tq = n_ctx
    return pl.pallas_call(
        functools.partial(_attn_kernel, n_ctx=n_ctx, n_all=t, lam_init=lam_init),
        grid=(b, t // tq),
        in_specs=[
            pl.BlockSpec((1, tq, w), lambda i, j: (i, j, 0)),
            pl.BlockSpec((1, t, w), lambda i, j: (i, 0, 0)),
            pl.BlockSpec((1, t, w), lambda i, j: (i, 0, 0)),
            pl.BlockSpec((4, HEAD_DIM), lambda i, j: (0, 0)),
            pl.BlockSpec((1, HEAD_W), lambda i, j: (0, 0)),
        ],
        out_specs=pl.BlockSpec((1, tq, w), lambda i, j: (i, j, 0)),
        out_shape=jax.ShapeDtypeStruct((b, t, w), BF16),
        compiler_params=_cparams(("arbitrary", "arbitrary")),
        name="diff_attn",
    )(q, k, v, lamv, gain)


def _gelu_tanh(x):
    return 0.5 * x * (1.0 + jnp.tanh(math.sqrt(2.0 / math.pi) * (x + 0.044715 * x * x * x)))


def _softplus(z):
    return jnp.maximum(z, 0.0) + jnp.log1p(jnp.exp(-jnp.abs(z)))


def _rg_kernel(xr_ref, y_ref, cw_ref, cb_ref, wg_ref, bias_ref, lam_ref, o_ref,
               xc_sc, acc_sc, a_sl, b_sl, *, n_ctx, n_all, chunk):
    t, c, r = n_all, n_ctx, chunk
    half = MXU_DIM
    rgw = xc_sc.shape[1]
    groups = rgw // LANES
    nseg = SUBLANES
    seg = r // nseg

    row = lax.broadcasted_iota(jnp.int32, (t, 1), 0)
    in_ctx = row < c
    tl = jnp.where(in_ctx, row, row - c)
    span = jnp.where(in_ctx, c, t - c)
    for g in range(groups):
        sl = slice(g * LANES, (g + 1) * LANES)
        xg = xr_ref[0, :, sl]
        wv = cw_ref[:, sl]
        acc = cb_ref[:, sl] + xg * wv[2:3]
        acc = acc + jnp.where(tl >= 2, pltpu.roll(xg, 2, 0), 0.0) * wv[0:1]
        acc = acc + jnp.where(tl >= 1, pltpu.roll(xg, 1, 0), 0.0) * wv[1:2]
        acc = acc + jnp.where(tl + 1 < span, pltpu.roll(xg, t - 1, 0), 0.0) * wv[3:4]
        xc_sc[:, sl] = acc

    n_chunks = t // r
    ctx_chunks = c // r
    for d in range(2):
        c4 = (-0.5 * RG_C) * _softplus(-lam_ref[d:d + 1, :])
        ba = 0.5 * bias_ref[2 * d:2 * d + 1, :]
        bx = 0.5 * bias_ref[2 * d + 1:2 * d + 2, :]

        def chunk_step(s, hc, d=d, c4=c4, ba=ba, bx=bx):
            if d == 0:
                ci = s
            else:
                ci = jnp.where(s < ctx_chunks, ctx_chunks - 1 - s,
                               n_chunks - 1 - (s - ctx_chunks))
            r0 = pl.multiple_of(ci * r, r)
            xc = xc_sc[pl.ds(r0, r), :]
            xb = xc.astype(BF16)
            for hh in range(rgw // half):
                hs = slice(hh * half, (hh + 1) * half)
                g2 = jnp.dot(xb[:, hs], wg_ref[d, hh], preferred_element_type=F32)
                tr = jnp.tanh(g2[:, :half] + ba[:, hs])
                gi = 0.5 * jnp.tanh(g2[:, half:] + bx[:, hs]) + 0.5
                log_a = c4[:, hs] * (tr + 1.0)
                a = jnp.exp(log_a)
                th = jnp.tanh(log_a)
                bt = jnp.sqrt(-2.0 * th) * lax.rsqrt(1.0 - th) * (gi * xc[:, hs])
                for gg in range(half // LANES):
                    g = hh * (half // LANES) + gg
                    ls = slice(gg * LANES, (gg + 1) * LANES)
                    for k in range(nseg):
                        rows = slice(k * seg, (k + 1) * seg)
                        a_sl[g, pl.ds(k, seg, stride=nseg), :] = a[rows, ls]
                        b_sl[g, pl.ds(k, seg, stride=nseg), :] = bt[rows, ls]

            def seg_step(ii, carry):
                i = ii if d == 0 else seg - 1 - ii
                base = pl.multiple_of(i * nseg, nseg)
                hs_, as_ = carry
                nh, na = [], []
                for g in range(groups):
                    av = a_sl[g, pl.ds(base, nseg), :]
                    hv = av * hs_[g] + b_sl[g, pl.ds(base, nseg), :]
                    pv = av * as_[g]
                    b_sl[g, pl.ds(base, nseg), :] = hv
                    a_sl[g, pl.ds(base, nseg), :] = pv
                    nh.append(hv)
                    na.append(pv)
                return tuple(nh), tuple(na)

            zero = jnp.zeros((nseg, LANES), F32)
            one = jnp.ones((nseg, LANES), F32)
            h_end, a_end = lax.fori_loop(
                0, seg, seg_step, ((zero,) * groups, (one,) * groups), unroll=4)

            order = range(nseg) if d == 0 else range(nseg - 1, -1, -1)
            new_carry = []
            for g in range(groups):
                ls = slice(g * LANES, (g + 1) * LANES)
                h_in = hc[:, ls]
                for k in order:
                    rows = pl.ds(r0 + k * seg, seg)
                    h_true = (b_sl[g, pl.ds(k, seg, stride=nseg), :]
                              + a_sl[g, pl.ds(k, seg, stride=nseg), :] * h_in)
                    if d == 0:
                        acc_sc[rows, ls] = h_true
                    else:
                        tot = acc_sc[rows, ls] + h_true
                        o_ref[0, rows, ls] = (tot * _gelu_tanh(y_ref[0, rows, ls])).astype(o_ref.dtype)
                    h_in = h_end[g][k:k + 1, :] + a_end[g][k:k + 1, :] * h_in
                new_carry.append(h_in)
            return jnp.concatenate(new_carry, axis=1)

        lax.fori_loop(0, n_chunks, chunk_step, jnp.zeros((1, rgw), F32))


def _rglru(xr, y, conv_w, conv_b, wg, bias, lam, n_ctx, chunk):
    b, t, w = xr.shape
    full = lambda a: pl.BlockSpec(a.shape, lambda i: (0,) * a.ndim)
    seq = pl.BlockSpec((1, t, w), lambda i: (i, 0, 0))
    return pl.pallas_call(
        functools.partial(_rg_kernel, n_ctx=n_ctx, n_all=t, chunk=chunk),
        grid=(b,),
        in_specs=[seq, seq, full(conv_w), full(conv_b), full(wg), full(bias), full(lam)],
        out_specs=seq,
        out_shape=jax.ShapeDtypeStruct((b, t, w), BF16),
        scratch_shapes=[
            pltpu.VMEM((t, w), F32),
            pltpu.VMEM((t, w), F32),
            pltpu.VMEM((w // LANES, chunk, LANES), F32),
            pltpu.VMEM((w // LANES, chunk, LANES), F32),
        ],
        compiler_params=_cparams(("arbitrary",)),
        name="rglru",
    )(xr, y, conv_w, conv_b, wg, bias, lam)


def _out_kernel(*refs, tm, n_ctx, alpha, moe):
    if moe:
        (att_ref, rg_ref, h_ref, mb_ref, mc_ref, w_ref, g_ref, b_ref, rt_ref,
         h1_ref, u_ref, route_ref) = refs
    else:
        att_ref, rg_ref, h_ref, mb_ref, mc_ref, w_ref, g_ref, b_ref, h1_ref = refs
    j = pl.program_id(1)
    row = j * tm + lax.broadcasted_iota(jnp.int32, (tm, 1), 0)
    is_ctx = row < n_ctx
    aw = att_ref.shape[2]
    mix = (jnp.dot(att_ref[0], w_ref[0:aw, :], preferred_element_type=F32)
           + jnp.dot(rg_ref[0], w_ref[aw:, :], preferred_element_type=F32))
    g1 = _row_mods(mb_ref, mc_ref, 2, is_ctx)
    h1 = _layer_norm(alpha * h_ref[0] + (1.0 + g1) * mix, g_ref[...], b_ref[...])
    h1_ref[0] = h1
    if not moe:
        return
    sh2 = _row_mods(mb_ref, mc_ref, 3, is_ctx)
    sc2 = _row_mods(mb_ref, mc_ref, 4, is_ctx)
    u = h1 * (1.0 + sc2) + sh2
    u_ref[0] = u
    n_exp = route_ref.shape[1]
    u_hi = u.astype(BF16)
    u_lo = (u - u_hi.astype(F32)).astype(BF16)
    prod = (jnp.dot(u_hi, rt_ref[...], preferred_element_type=F32)
            + jnp.dot(u_lo, rt_ref[...], preferred_element_type=F32))
    logits = prod[:, :LANES] + prod[:, LANES:]
    lane = lax.broadcasted_iota(jnp.int32, (1, LANES), 1)
    neg = -jnp.inf
    lg = jnp.where(lane < n_exp, logits, neg)
    m1 = jnp.max(lg, -1, keepdims=True)
    i1 = jnp.min(jnp.where(lg == m1, lane, LANES), -1, keepdims=True)
    lg2 = jnp.where(lane == i1, neg, lg)
    m2 = jnp.max(lg2, -1, keepdims=True)
    i2 = jnp.min(jnp.where(lg2 == m2, lane, LANES), -1, keepdims=True)
    e2 = jnp.exp(m2 - m1)
    den = 1.0 + e2
    route = jnp.where(lane == 0, i1.astype(F32),
                      jnp.where(lane == 1, i2.astype(F32),
                                jnp.where(lane == 2, 1.0 / den,
                                          jnp.where(lane == 3, e2 / den, 0.0))))
    route_ref[...] = route[:, :n_exp]


def _out_proj(att, rg, h, mod, w_out, ln_g, ln_b, router, n_ctx, tm, alpha):
    b, t, d = h.shape
    aw = att.shape[2]
    moe = router is not None
    row_spec = lambda width: pl.BlockSpec((1, tm, width), lambda i, j: (i, j, 0))
    vec = pl.BlockSpec((1, d), lambda i, j: (0, 0))
    in_specs = [
        row_spec(aw), row_spec(aw), row_spec(d),
        pl.BlockSpec((1, 6, d), lambda i, j: (i, 0, 0)),
        pl.BlockSpec((1, 6, d), lambda i, j: (b, 0, 0)),
        pl.BlockSpec((d, d), lambda i, j: (0, 0)),
        vec, vec,
    ]
    args = [att, rg, h, mod, mod, w_out, ln_g, ln_b]
    out_specs = [row_spec(d)]
    out_shape = [jax.ShapeDtypeStruct((b, t, d), F32)]
    if moe:
        n_exp = router.shape[1]
        r_hi = router.astype(BF16)
        r_lo = (router - r_hi.astype(F32)).astype(BF16)
        pad = lambda a: jnp.zeros((d, LANES), BF16).at[:, :n_exp].set(a)
        in_specs.append(pl.BlockSpec((d, 2 * LANES), lambda i, j: (0, 0)))
        args.append(jnp.concatenate([pad(r_hi), pad(r_lo)], axis=1))
        tiles = t // tm
        out_specs += [row_spec(d), pl.BlockSpec((tm, n_exp), lambda i, j: (i * tiles + j, 0))]
        out_shape += [jax.ShapeDtypeStruct((b, t, d), F32),
                      jax.ShapeDtypeStruct((b * t, n_exp), F32)]
    return pl.pallas_call(
        functools.partial(_out_kernel, tm=tm, n_ctx=n_ctx, alpha=alpha, moe=moe),
        grid=(b, t // tm),
        in_specs=in_specs,
        out_specs=out_specs,
        out_shape=out_shape,
        compiler_params=_cparams(("arbitrary", "arbitrary")),
        name="out_proj_moe" if moe else "out_proj",
    )(*args)


def _swiglu(x, w1_ref, w3_ref, w2_ref, fc):
    f = w1_ref.shape[2]
    acc = None
    for c in range(f // fc):
        cs = slice(c * fc, (c + 1) * fc)
        a = jnp.dot(x, w1_ref[0, :, cs], preferred_element_type=F32)
        bgate = jnp.dot(x, w3_ref[0, :, cs], preferred_element_type=F32)
        gact = (a * _sigmoid(a) * bgate).astype(BF16)
        part = jnp.dot(gact, w2_ref[0, cs, :], preferred_element_type=F32)
        acc = part if acc is None else acc + part
    return acc


def _ffn_chunk(f):
    return MXU_DIM if f % MXU_DIM == 0 else f


def _dense_ffn_kernel(h1_ref, mb_ref, mc_ref, w1_ref, w3_ref, w2_ref, g_ref, b_ref, o_ref,
                      *, tf, n_ctx, alpha, fc):
    row = pl.program_id(1) * tf + lax.broadcasted_iota(jnp.int32, (tf, 1), 0)
    is_ctx = row < n_ctx
    sh2 = _row_mods(mb_ref, mc_ref, 3, is_ctx)
    sc2 = _row_mods(mb_ref, mc_ref, 4, is_ctx)
    g2 = _row_mods(mb_ref, mc_ref, 5, is_ctx)
    h1 = h1_ref[0]
    u = (h1 * (1.0 + sc2) + sh2).astype(BF16)
    mixed = _swiglu(u, w1_ref, w3_ref, w2_ref, fc)
    o_ref[0] = _layer_norm(alpha * h1 + (1.0 + g2) * mixed, g_ref[...], b_ref[...])


def _dense_ffn(h1, mod, w1, w3, w2, ln_g, ln_b, n_ctx, tf, alpha):
    b, t, d = h1.shape
    f = w1.shape[2]
    row_spec = pl.BlockSpec((1, tf, d), lambda i, j: (i, j, 0))
    vec = pl.BlockSpec((1, d), lambda i, j: (0, 0))
    const3 = lambda a: pl.BlockSpec(a.shape, lambda i, j: (0, 0, 0))
    return pl.pallas_call(
        functools.partial(_dense_ffn_kernel, tf=tf, n_ctx=n_ctx, alpha=alpha, fc=_ffn_chunk(f)),
        grid=(b, t // tf),
        in_specs=[
            row_spec,
            pl.BlockSpec((1, 6, d), lambda i, j: (i, 0, 0)),
            pl.BlockSpec((1, 6, d), lambda i, j: (b, 0, 0)),
            const3(w1), const3(w3), const3(w2), vec, vec,
        ],
        out_specs=row_spec,
        out_shape=jax.ShapeDtypeStruct((b, t, d), F32),
        compiler_params=_cparams(("arbitrary", "arbitrary")),
        name="dense_ffn",
    )(h1, mod, mod, w1, w3, w2, ln_g, ln_b)


def _moe_ffn_kernel(te_ref, nu_ref, src_ref, nxt_ref, u_ref, w1_ref, w3_ref, w2_ref, o_ref,
                    xbuf, sem, *, tm, fc):
    del te_ref
    i = pl.program_id(0)
    n_used = nu_ref[0]
    slot = i % 2

    def gather(idx_ref, dst_slot):
        for r in range(tm):
            pltpu.make_async_copy(u_ref.at[pl.ds(idx_ref[0, 0, r], 1)],
                                  xbuf.at[dst_slot, pl.ds(r, 1)], sem.at[dst_slot]).start()

    @pl.when(jnp.logical_and(i == 0, n_used > 0))
    def _():
        gather(src_ref, 0)

    @pl.when(i + 1 < n_used)
    def _():
        gather(nxt_ref, 1 - slot)

    @pl.when(i < n_used)
    def _():
        pltpu.make_async_copy(u_ref.at[pl.ds(0, tm)], xbuf.at[slot], sem.at[slot]).wait()
        o_ref[...] = _swiglu(xbuf[slot].astype(BF16), w1_ref, w3_ref, w2_ref, fc)

    @pl.when(i >= n_used)
    def _():
        o_ref[...] = jnp.zeros_like(o_ref)


def _moe_ffn(u, src, w1, w3, w2, layer, tile_expert, n_used, tm):
    n, d = u.shape
    f = w1.shape[3]
    n_tiles = src.shape[0] // tm
    src3 = src.reshape(n_tiles, 1, tm)
    wspec = lambda shape: pl.BlockSpec((None,) + shape, lambda i, te, nu: (layer, te[i], 0, 0))
    grid_spec = pltpu.PrefetchScalarGridSpec(
        num_scalar_prefetch=2,
        grid=(n_tiles,),
        in_specs=[
            pl.BlockSpec((1, 1, tm), lambda i, te, nu: (i, 0, 0), memory_space=pltpu.SMEM),
            pl.BlockSpec((1, 1, tm), lambda i, te, nu: (jnp.minimum(i + 1, n_tiles - 1), 0, 0),
                         memory_space=pltpu.SMEM),
            pl.BlockSpec(memory_space=pl.ANY),
            wspec((1, d, f)), wspec((1, d, f)), wspec((1, f, d)),
        ],
        out_specs=pl.BlockSpec((tm, d), lambda i, te, nu: (i, 0)),
        scratch_shapes=[pltpu.VMEM((2, tm, d), F32), pltpu.SemaphoreType.DMA((2,))],
    )
    return pl.pallas_call(
        functools.partial(_moe_ffn_kernel, tm=tm, fc=_ffn_chunk(f)),
        grid_spec=grid_spec,
        out_shape=jax.ShapeDtypeStruct((n_tiles * tm, d), F32),
        compiler_params=pltpu.CompilerParams(
            dimension_semantics=("arbitrary",), vmem_limit_bytes=VMEM_LIMIT,
            allow_input_fusion=[False] * 5 + [True] * 3),
        name="moe_ffn",
    )(tile_expert, n_used, src3, src3, u, w1, w3, w2)


def _combine_kernel(dst_ref, nxt_ref, ys_ref, route_ref, h1_ref, mb_ref, mc_ref, g_ref, b_ref,
                    o_ref, buf, sem, *, tc, n_ctx, first_tile, alpha):
    i, j = pl.program_id(0), pl.program_id(1)
    nj = pl.num_programs(1)
    s = i * nj + j
    total = pl.num_programs(0) * nj
    slot = s % 2

    def gather(idx_ref, dslot):
        for r in range(tc):
            for k in range(N_TOP):
                pltpu.make_async_copy(ys_ref.at[pl.ds(idx_ref[0, 0, N_TOP * r + k], 1)],
                                      buf.at[dslot, pl.ds(k * tc + r, 1)], sem.at[dslot]).start()

    @pl.when(s == 0)
    def _():
        gather(dst_ref, 0)

    @pl.when(s + 1 < total)
    def _():
        gather(nxt_ref, 1 - slot)

    pltpu.make_async_copy(ys_ref.at[pl.ds(0, N_TOP * tc)], buf.at[slot], sem.at[slot]).wait()
    rt = route_ref[...]
    mixed = rt[:, 2:3] * buf[slot, 0:tc] + rt[:, 3:4] * buf[slot, tc:N_TOP * tc]
    row = (first_tile + j) * tc + lax.broadcasted_iota(jnp.int32, (tc, 1), 0)
    g2 = _row_mods(mb_ref, mc_ref, 5, row < n_ctx)
    o_ref[0] = _layer_norm(alpha * h1_ref[0] + (1.0 + g2) * mixed, g_ref[...], b_ref[...])


def _moe_combine(h1, mod, ln_g, ln_b, ys, dst, route, n_ctx, tc, alpha, skip_ctx):
    b, t, d = h1.shape
    tiles = t // tc
    off = n_ctx // tc if skip_ctx else 0
    nj = tiles - off
    n_exp = route.shape[1]
    dst3 = dst.reshape(b * tiles, 1, N_TOP * tc)

    def cur(i, j):
        return i * tiles + off + j

    def nxt(i, j):
        i2 = jnp.minimum(i + (j + 1) // nj, b - 1)
        return i2 * tiles + off + (j + 1) % nj

    vec = pl.BlockSpec((1, d), lambda i, j: (0, 0))
    return pl.pallas_call(
        functools.partial(_combine_kernel, tc=tc, n_ctx=n_ctx, first_tile=off, alpha=alpha),
        grid=(b, nj),
        in_specs=[
            pl.BlockSpec((1, 1, N_TOP * tc), lambda i, j: (cur(i, j), 0, 0),
                         memory_space=pltpu.SMEM),
            pl.BlockSpec((1, 1, N_TOP * tc), lambda i, j: (nxt(i, j), 0, 0),
                         memory_space=pltpu.SMEM),
            pl.BlockSpec(memory_space=pl.ANY),
            pl.BlockSpec((tc, n_exp), lambda i, j: (cur(i, j), 0)),
            pl.BlockSpec((1, tc, d), lambda i, j: (i, off + j, 0)),
            pl.BlockSpec((1, 6, d), lambda i, j: (i, 0, 0)),
            pl.BlockSpec((1, 6, d), lambda i, j: (b, 0, 0)),
            vec, vec,
        ],
        out_specs=pl.BlockSpec((1, tc, d), lambda i, j: (i, j, 0)),
        out_shape=jax.ShapeDtypeStruct((b, nj * tc, d), F32),
        scratch_shapes=[pltpu.VMEM((2, N_TOP * tc, d), F32), pltpu.SemaphoreType.DMA((2,))],
        compiler_params=_cparams(("arbitrary", "arbitrary")),
        name="moe_combine",
    )(dst3, dst3, ys, route, h1, mod, mod, ln_g, ln_b)


def _rope_tables(n_ctx, seq):
    rows = seq // GRID_W
    pairs = HEAD_DIM // 4
    rpos = jnp.repeat(jnp.arange(rows, dtype=F32), GRID_W)
    cpos = jnp.tile(jnp.arange(GRID_W, dtype=F32), rows)
    inv_freq = ROPE_BASE ** (-jnp.arange(pairs, dtype=F32) / pairs)
    ang = jnp.concatenate([rpos[:, None] * inv_freq, cpos[:, None] * inv_freq], -1)
    cos, sin = jnp.cos(ang), jnp.sin(ang)
    reps = LANES // HEAD_DIM
    cs = jnp.tile(jnp.concatenate([cos, cos], -1), (1, reps))
    sn = jnp.tile(jnp.concatenate([-sin, sin], -1), (1, reps))
    cs = jnp.concatenate([jnp.ones((n_ctx, LANES), F32), cs], 0)
    sn = jnp.concatenate([jnp.zeros((n_ctx, LANES), F32), sn], 0)
    return cs, sn


def _prep_w_in(w_in):
    lead = w_in.shape[:-1]
    q0 = 3 * ATTN_W

    def split_pairs(w):
        w = w.reshape(*lead, ATTN_W // HEAD_DIM, HEAD_DIM // 2, 2)
        return jnp.swapaxes(w, -1, -2).reshape(*lead, ATTN_W)

    q_scale = HEAD_DIM ** -0.5 * math.log2(math.e)
    return jnp.concatenate([
        split_pairs(w_in[..., :ATTN_W]), w_in[..., ATTN_W:q0],
        split_pairs(w_in[..., q0:q0 + ATTN_W]) * q_scale, w_in[..., q0 + ATTN_W:]],
        axis=-1).astype(BF16)


def _prep_rg_gates(wa, wx):
    nd, g, bw, _ = wa.shape
    per = MXU_DIM // bw
    halves = g // per

    def bdiag(w):
        w = w.reshape(nd, halves, per, bw, bw)
        eye = jnp.eye(per, dtype=w.dtype)
        full = w[:, :, :, :, None, :] * eye[None, None, :, None, :, None]
        return full.reshape(nd, halves, per * bw, per * bw)

    return (0.5 * jnp.concatenate([bdiag(wa), bdiag(wx)], -1)).astype(BF16)


def _routing_tables(route, keep, n_exp, tm):
    e = route[:, :N_TOP].astype(jnp.int32).reshape(-1)
    keep2 = jnp.repeat(keep, N_TOP)
    oh = ((e[:, None] == jnp.arange(n_exp)[None, :]) & keep2[:, None]).astype(jnp.int32)
    csum = jnp.cumsum(oh, axis=0)
    rank = jnp.sum((csum - oh) * oh, axis=1)
    counts = csum[-1]
    tiles_per = (counts + tm - 1) // tm
    tile_end = jnp.cumsum(tiles_per)
    off = (tile_end - tiles_per) * tm
    n_tiles = e.shape[0] // tm + n_exp
    rows = n_tiles * tm
    dst = jnp.where(keep2, jnp.sum(oh * off[None, :], axis=1) + rank, rows).astype(jnp.int32)
    tile_ids = jnp.arange(n_tiles)
    tile_expert = jnp.sum((tile_ids[:, None] >= tile_end[None, :]).astype(jnp.int32), axis=1)
    tile_expert = jnp.minimum(tile_expert, n_exp - 1).astype(jnp.int32)
    bits = (e.shape[0] - 1).bit_length()
    keys = (jnp.where(keep2, e, n_exp) << bits) | jnp.arange(e.shape[0], dtype=jnp.int32)
    order = jnp.sort(keys) & ((1 << bits) - 1)
    row_expert = jnp.repeat(tile_expert, tm)
    k_in = jnp.arange(rows, dtype=jnp.int32) - off[row_expert]
    first = jnp.cumsum(counts) - counts
    slot = jnp.clip(first[row_expert] + k_in, 0, e.shape[0] - 1)
    src = jnp.where(k_in < counts[row_expert], order[slot] // N_TOP, 0).astype(jnp.int32)
    return dst, src, tile_expert, tile_end[-1:].astype(jnp.int32)


def kernel(x, c, ctx, c_ctx, w_mod, b_mod, w_in, lam_q1, lam_k1, lam_q2, lam_k2, subln_g,
           conv_w, conv_b, rg_wa, rg_ba, rg_wx, rg_bx, rg_lambda, w_out,
           ln1_g, ln1_b, ln2_g, ln2_b, ffn_w1, ffn_w3, ffn_w2,
           moe_router, moe_w1, moe_w3, moe_w2):
    b, seq, d = x.shape
    n_ctx = ctx.shape[1]
    t = n_ctx + seq
    depth = w_in.shape[0]
    alpha = (2 * depth) ** 0.25
    assert seq % GRID_W == 0 and d - ATTN_W == ATTN_W and t % n_ctx == 0

    tm = _pick_tile(t, 768)
    tf = _pick_tile(t, 768)
    te = 512 if (N_TOP * b * t) % 512 == 0 else _pick_tile(N_TOP * b * t, 512)
    tc = _pick_tile(n_ctx, 256)
    chunk = math.gcd(n_ctx, 256)

    h = jnp.concatenate([ctx, x], axis=1)
    mrows = -(-(b + 1) // SUBLANES) * SUBLANES
    cvec = jnp.zeros((mrows, d), F32).at[:b].set(c).at[b].set(c_ctx)
    mod_all = _modulation(cvec, w_mod, b_mod).reshape(depth, mrows, 6, d)
    cs_tab, sn_tab = _rope_tables(n_ctx, seq)

    w_in_p = _prep_w_in(w_in)
    w_out_b = w_out.astype(BF16)
    moe_w1b, moe_w3b, moe_w2b = (w.astype(BF16) for w in (moe_w1, moe_w3, moe_w2))
    lamv =jnp.stack([lam_q1, lam_k1, lam_q2, lam_k2], axis=1)
    rg_bias = jnp.stack([rg_ba, rg_bx], axis=2).reshape(depth, 4, -1)
    is_lat = jnp.tile(jnp.arange(t) >= n_ctx, b)

    for i in range(depth):
        last = i == depth - 1
        lam_init = 0.8 - 0.6 * math.exp(-0.3 * i)
        mod = mod_all[i]
        g2_ln = (ln2_g[i][None, :], ln2_b[i][None, :])
        k, v, xr, q, y = _in_proj(h, mod, w_in_p[i], cs_tab, sn_tab, n_ctx, tm)
        att = _attention(q, k, v, lamv[i], subln_g[i][None, :], n_ctx, lam_init)
        rg = _rglru(xr, y, conv_w[i], conv_b[i][None, :], _prep_rg_gates(rg_wa[i], rg_wx[i]),
                    rg_bias[i], rg_lambda[i], n_ctx, chunk)
        j = i // 2
        router = moe_router[j] if i % 2 else None
        outs = _out_proj(att, rg, h, mod, w_out_b[i], ln1_g[i][None, :], ln1_b[i][None, :],
                         router, n_ctx, tm, alpha)
        h1 = outs[0]
        if i % 2 == 0:
            h = _dense_ffn(h1, mod, ffn_w1[j][None].astype(BF16), ffn_w3[j][None].astype(BF16),
                           ffn_w2[j][None].astype(BF16), *g2_ln, n_ctx, tf, alpha)
        else:
            u, route = outs[1], outs[2]
            n_exp = route.shape[1]
            keep = is_lat if last else jnp.ones((b * t,), bool)
            dst, src, tile_expert, n_used = _routing_tables(route, keep, n_exp, te)
            ys = _moe_ffn(u.reshape(b * t, d), src, moe_w1b, moe_w3b, moe_w2b, j,
                          tile_expert, n_used, te)
            h = _moe_combine(h1, mod, *g2_ln, ys, dst, route, n_ctx, tc, alpha, skip_ctx=last)
    return h if h.shape[1] == seq else h[:, n_ctx:, :]
```
